```python
import jax, jax.numpy as jnp
from jax import lax
import numpy as np

D_MODEL = 1024
BATCH = 32
SEQ = 2048
DEPTH = 1

CTX_LEN = 256
GRID_W = 64
N_SUB = 3
N_MOD = 3 * N_SUB
FFN_HIDDEN = 2816
GLA_HEADS = 4
GLA_DK = 128
GLA_DV = 256
GLA_KEY = GLA_HEADS * GLA_DK
GLA_VAL = GLA_HEADS * GLA_DV
GLA_RANK = 16
GLA_TAU = 16.0
GLA_CHUNK = 64
LRU_WIDTH = 1024
LRU_BLOCKS = 8
LRU_BW = LRU_WIDTH // LRU_BLOCKS
LRU_C = 8.0
CONV_K = 4
EPS = 1e-6
IN_SIZES = (GLA_KEY, GLA_KEY, GLA_VAL, GLA_VAL, GLA_RANK, GLA_RANK,
            LRU_WIDTH, LRU_WIDTH, D_MODEL, D_MODEL)
IN_WIDTH = 2 * GLA_KEY + 2 * GLA_VAL + 2 * GLA_RANK + 2 * LRU_WIDTH + 2 * D_MODEL

kernel_name = "hybrid_gla_rglru_macaron_prefix_block"


def rmsnorm(x, w):
    xf = x.astype(jnp.float32)
    y = xf * lax.rsqrt(jnp.mean(xf * xf, axis=-1, keepdims=True) + EPS)
    return (y * w.astype(jnp.float32)).astype(x.dtype)


def modulate(h, w, shift, scale):
    return rmsnorm(h, w) * (1.0 + scale) + shift


def swiglu(u, wi, wo):
    gate, up = jnp.split(u @ wi, 2, axis=-1)
    return (jax.nn.silu(gate) * up) @ wo


def ffn_half(t, m, s, nw, wi, wo):
    u = modulate(t, nw, m[3 * s], m[3 * s + 1])
    return t + 0.5 * m[3 * s + 2] * swiglu(u, wi, wo)


def flip(t):
    return jnp.flip(t, axis=1)


def dwconv_centred(t, w, b):
    L = t.shape[1]
    left = CONV_K // 2
    tp = jnp.pad(t, ((0, 0), (left, CONV_K - 1 - left), (0, 0)))
    out = b
    for j in range(CONV_K):
        out = out + tp[:, j:j + L] * w[j]
    return out


def gla_chunk(q, k, v, log_a, s0, strict):
    Bn, T, H, DK = q.shape
    DV = v.shape[-1]
    n = T // GLA_CHUNK
    q = q.reshape(Bn, n, GLA_CHUNK, H, DK)
    k = k.reshape(Bn, n, GLA_CHUNK, H, DK)
    v = v.reshape(Bn, n, GLA_CHUNK, H, DV)
    b = jnp.cumsum(log_a.reshape(Bn, n, GLA_CHUNK, H, DK), axis=2)
    b_last = b[:, :, -1:]
    q_dec = q * jnp.exp(b)
    k_inv = k * jnp.exp(-b)
    k_end = k * jnp.exp(b_last - b)
    scores = jnp.einsum('bnihd,bnjhd->bnhij', q_dec, k_inv)
    mask = jnp.tril(jnp.ones((GLA_CHUNK, GLA_CHUNK), dtype=bool), -1 if strict else 0)
    scores = jnp.where(mask, scores, 0.0)
    o_intra = jnp.einsum('bnhij,bnjhe->bnihe', scores, v)

    def step(s, inp):
        qd, ke, vv, bl = inp
        o = jnp.einsum('bihd,bhde->bihe', qd, s)
        s = s * jnp.exp(bl)[..., None] + jnp.einsum('bjhd,bjhe->bhde', ke, vv)
        return s, o

    xs = (jnp.moveaxis(q_dec, 1, 0), jnp.moveaxis(k_end, 1, 0),
          jnp.moveaxis(v, 1, 0), jnp.moveaxis(b_last[:, :, 0], 1, 0))
    s_fin, o_inter = lax.scan(step, s0, xs)
    o = o_intra + jnp.moveaxis(o_inter, 0, 1)
    return o.reshape(Bn, T, H, DV), s_fin


def rglru_scan(xc, w_r, b_r, w_i, b_i, lam, h0):
    Bn, T, W = xc.shape
    xb = xc.reshape(Bn, T, LRU_BLOCKS, LRU_BW)
    r = jax.nn.sigmoid(jnp.einsum('btnc,ncd->btnd', xb, w_r).reshape(Bn, T, W) + b_r)
    i = jax.nn.sigmoid(jnp.einsum('btnc,ncd->btnd', xb, w_i).reshape(Bn, T, W) + b_i)
    log_a = -LRU_C * r * jax.nn.softplus(-lam)
    a = jnp.exp(log_a)
    u = jnp.sqrt(-jnp.expm1(2.0 * log_a)) * (i * xc)

    def combine(e1, e2):
        a1, b1 = e1
        a2, b2 = e2
        return a1 * a2, a2 * b1 + b2

    a_cum, h_zero = lax.associative_scan(combine, (a, u), axis=1)
    h = a_cum * h0[:, None] + h_zero
    return h, h[:, -1]


def token_mixer(proj, conv_fn, init, lp, need_out):
    (fup, fb, gnorm_w, conv_w, conv_b, lru_wr, lru_br, lru_wi, lru_bi, lru_lam,
     w_out_gla, w_out_lru, w_o) = lp
    dt = proj.dtype
    split_idx = [int(s) for s in np.cumsum(IN_SIZES)[:-1]]
    q, k, v, g, fd_f, fd_b, xl, yl, ga, gb = jnp.split(proj, split_idx, axis=-1)
    Bn, T, _ = proj.shape
    f32 = jnp.float32

    def heads(t, d):
        return t.astype(f32).reshape(Bn, T, GLA_HEADS, d)

    qh = heads(q, GLA_DK) * (GLA_DK ** -0.5)
    kh = heads(k, GLA_DK)
    vh = heads(v, GLA_DV)
    la_f = heads(jax.nn.log_sigmoid(fd_f @ fup[0] + fb[0]), GLA_DK) / GLA_TAU
    la_b = heads(jax.nn.log_sigmoid(fd_b @ fup[1] + fb[1]), GLA_DK) / GLA_TAU
    s_f0, s_b0, h_f0, h_b0 = init
    o_f, s_f = gla_chunk(qh, kh, vh, la_f, s_f0, False)
    o_b, s_b = gla_chunk(flip(qh), flip(kh), flip(vh), flip(la_b), s_b0, True)

    xc = conv_fn(xl, conv_w, conv_b).astype(f32)
    hf, h_f = rglru_scan(xc, lru_wr[0], lru_br[0], lru_wi[0], lru_bi[0], lru_lam[0], h_f0)
    hb, h_b = rglru_scan(flip(xc), lru_wr[1], lru_br[1], lru_wi[1], lru_bi[1], lru_lam[1], h_b0)
    states = (s_f, s_b, h_f, h_b)
    if not need_out:
        return None, states

    o = o_f + flip(o_b)
    o = rmsnorm(o, gnorm_w) * jax.nn.silu(heads(g, GLA_DV))
    y_gla = o.reshape(Bn, T, GLA_VAL).astype(dt) @ w_out_gla
    y_lru = ((hf + flip(hb)).astype(dt) * jax.nn.gelu(yl)) @ w_out_lru
    merged = jax.nn.sigmoid(ga) * y_gla + jax.nn.sigmoid(gb) * y_lru
    return merged @ w_o, states


def setup_inputs(seed: int = 0) -> dict:
    key = jax.random.key(seed)
    ks = jax.random.split(key, 32)
    nrm = jax.random.normal
    D = D_MODEL
    p = (jax.random.uniform(ks[25], (DEPTH, 2, LRU_WIDTH), minval=0.9, maxval=0.999)) ** (1.0 / LRU_C)
    return {
        "x": nrm(ks[0], (BATCH, SEQ, D)),
        "c": nrm(ks[1], (BATCH, D)),
        "ctx": nrm(ks[2], (BATCH, CTX_LEN, D)),
        "c_ctx": nrm(ks[3], (D,)),
        "w_ada": nrm(ks[4], (DEPTH, D, N_MOD * D)) * (0.5 * D ** -0.5),
        "b_ada": nrm(ks[5], (DEPTH, N_MOD * D)) * 0.02,
        "norm_w": 1.0 + 0.05 * nrm(ks[6], (DEPTH, N_SUB, D)),
        "ffn1_wi": nrm(ks[7], (DEPTH, D, 2 * FFN_HIDDEN)) * D ** -0.5,
        "ffn1_wo": nrm(ks[8], (DEPTH, FFN_HIDDEN, D)) * FFN_HIDDEN ** -0.5,
        "ffn2_wi": nrm(ks[9], (DEPTH, D, 2 * FFN_HIDDEN)) * D ** -0.5,
        "ffn2_wo": nrm(ks[10], (DEPTH, FFN_HIDDEN, D)) * FFN_HIDDEN ** -0.5,
        "w_in": nrm(ks[11], (DEPTH, D, IN_WIDTH)) * D ** -0.5,
        "gla_fup": nrm(ks[12], (DEPTH, 2, GLA_RANK, GLA_KEY)) * GLA_RANK ** -0.5,
        "gla_fb": nrm(ks[13], (DEPTH, 2, GLA_KEY)) * 0.1,
        "gla_norm_w": 1.0 + 0.05 * nrm(ks[14], (DEPTH, GLA_DV)),
        "conv_w": nrm(ks[15], (DEPTH, CONV_K, LRU_WIDTH)) * CONV_K ** -0.5,
        "conv_b": nrm(ks[16], (DEPTH, LRU_WIDTH)) * 0.02,
        "lru_wr": nrm(ks[17], (DEPTH, 2, LRU_BLOCKS, LRU_BW, LRU_BW)) * LRU_BW ** -0.5,
        "lru_br": nrm(ks[18], (DEPTH, 2, LRU_WIDTH)) * 0.02,
        "lru_wi": nrm(ks[19], (DEPTH, 2, LRU_BLOCKS, LRU_BW, LRU_BW)) * LRU_BW ** -0.5,
        "lru_bi": nrm(ks[20], (DEPTH, 2, LRU_WIDTH)) * 0.02,
        "lru_lam": jnp.log(p) - jnp.log1p(-p),
        "w_out_gla": nrm(ks[21], (DEPTH, GLA_VAL, D)) * GLA_VAL ** -0.5,
        "w_out_lru": nrm(ks[22], (DEPTH, LRU_WIDTH, D)) * LRU_WIDTH ** -0.5,
        "w_o": nrm(ks[23], (DEPTH, D, D)) * D ** -0.5,
        "final_norm_w": 1.0 + 0.05 * nrm(ks[24], (D,)),
    }


def reference(x, c, ctx, c_ctx, w_ada, b_ada, norm_w, ffn1_wi, ffn1_wo, ffn2_wi, ffn2_wo,
              w_in, gla_fup, gla_fb, gla_norm_w, conv_w, conv_b, lru_wr, lru_br, lru_wi,
              lru_bi, lru_lam, w_out_gla, w_out_lru, w_o, final_norm_w):
    Bn, T, _ = x.shape
    rows = T // GRID_W

    def lat_conv(t, w, b):
        return dwconv_centred(t.reshape(Bn * rows, GRID_W, t.shape[-1]), w, b).reshape(Bn, T, t.shape[-1])

    f32 = jnp.float32
    zero_init = (jnp.zeros((Bn, GLA_HEADS, GLA_DK, GLA_DV), f32),
                 jnp.zeros((Bn, GLA_HEADS, GLA_DK, GLA_DV), f32),
                 jnp.zeros((Bn, LRU_WIDTH), f32),
                 jnp.zeros((Bn, LRU_WIDTH), f32))
    h, hc = x, ctx
    for l in range(DEPTH):
        last = l == DEPTH - 1
        mod = (jax.nn.silu(c) @ w_ada[l] + b_ada[l]).reshape(Bn, N_MOD, 1, D_MODEL)
        mod_c = (jax.nn.silu(c_ctx) @ w_ada[l] + b_ada[l]).reshape(N_MOD, D_MODEL)
        lat_m = [mod[:, j] for j in range(N_MOD)]
        ctx_m = [mod_c[j] for j in range(N_MOD)]
        lp = (gla_fup[l], gla_fb[l], gla_norm_w[l], conv_w[l], conv_b[l], lru_wr[l], lru_br[l],
              lru_wi[l], lru_bi[l], lru_lam[l], w_out_gla[l], w_out_lru[l], w_o[l])

        h = ffn_half(h, lat_m, 0, norm_w[l, 0], ffn1_wi[l], ffn1_wo[l])
        hc = ffn_half(hc, ctx_m, 0, norm_w[l, 0], ffn1_wi[l], ffn1_wo[l])

        uc = modulate(hc, norm_w[l, 1], ctx_m[3], ctx_m[4]) @ w_in[l]
        yc, ctx_states = token_mixer(uc, dwconv_centred, zero_init, lp, not last)
        u = modulate(h, norm_w[l, 1], lat_m[3], lat_m[4]) @ w_in[l]
        y, _ = token_mixer(u, lat_conv, ctx_states, lp, True)
        h = h + lat_m[5] * y

        h = ffn_half(h, lat_m, 2, norm_w[l, 2], ffn2_wi[l], ffn2_wo[l])
        if not last:
            hc = hc + ctx_m[5] * yc
            hc = ffn_half(hc, ctx_m, 2, norm_w[l, 2], ffn2_wi[l], ffn2_wo[l])
    return rmsnorm(h, final_norm_w)
```

```python
import functools

import jax
import jax.numpy as jnp
from jax import lax
from jax.experimental import pallas as pl
from jax.experimental.pallas import tpu as pltpu

F32 = jnp.float32
BF16 = jnp.bfloat16

N_MOD = 9
GLA_HEADS = 4
GLA_DK = 128
GLA_DV = 256
GLA_KEY = GLA_HEADS * GLA_DK
GLA_VAL = GLA_HEADS * GLA_DV
GLA_RANK = 16
GLA_TAU = 16.0
GLA_CHUNK = 64
GRID_W = 64
LRU_BLOCKS = 8
LRU_C = 8.0
CONV_K = 4
EPS = 1e-6

LANE = 128
SUBLANE = 8
VMEM_LIMIT = 56 * 1024 * 1024


def _cparams(*sem):
    return pltpu.CompilerParams(dimension_semantics=sem, vmem_limit_bytes=VMEM_LIMIT)


def _const_spec(shape):
    nd = len(shape)
    return pl.BlockSpec(shape, lambda *_: (0,) * nd, pipeline_mode=pl.Buffered(1))


def _dot(a, b):
    return jnp.dot(a, b, preferred_element_type=F32)


def _sigmoid(x):
    return 1.0 / (1.0 + jnp.exp(-x))


def _softplus(x):
    return jnp.maximum(x, 0.0) + jnp.log1p(jnp.exp(-jnp.abs(x)))


def _gelu_tanh(x):
    c = 0.7978845608028654
    return 0.5 * x * (1.0 + jnp.tanh(c * (x + 0.044715 * (x * x * x))))


def _rms(x):
    return x * lax.rsqrt(jnp.mean(x * x, axis=-1, keepdims=True) + EPS)


def _pick_tile(n, target, quantum):
    best = None
    t = quantum
    while t <= min(n, target):
        if n % t == 0:
            best = t
        t += quantum
    assert best is not None, (n, target, quantum)
    return best


def _ada_kernel(c_ref, w_ref, b_ref, o_ref):
    c = c_ref[...]
    s = (c * _sigmoid(c)).astype(BF16)
    o_ref[...] = _dot(s, w_ref[...].astype(BF16)) + b_ref[...]


def _ada(cs, w_ada, b_ada):
    R, D = cs.shape
    N = w_ada.shape[1]
    tn = _pick_tile(N, 2304, LANE)
    return pl.pallas_call(
        _ada_kernel,
        out_shape=jax.ShapeDtypeStruct((R, N), F32),
        grid=(N // tn,),
        in_specs=[pl.BlockSpec((R, D), lambda j: (0, 0)),
                  pl.BlockSpec((D, tn), lambda j: (0, j)),
                  pl.BlockSpec((1, tn), lambda j: (0, j))],
        out_specs=pl.BlockSpec((R, tn), lambda j: (0, j)),
        compiler_params=_cparams("parallel"),
        name="ada",
    )(cs, w_ada, b_ada.reshape(1, N))


def _ffn_kernel(x_ref, mod_ref, nw_ref, wi_ref, wo_ref, fnw_ref, o_ref, *, sub, n_chunks, final_norm):
    F = wo_ref.shape[0]
    tf = F // n_chunks
    x = x_ref[...]
    shift = mod_ref[3 * sub:3 * sub + 1, :]
    scale = mod_ref[3 * sub + 1:3 * sub + 2, :]
    gate = mod_ref[3 * sub + 2:3 * sub + 3, :]
    u = (_rms(x) * nw_ref[...] * (1.0 + scale) + shift).astype(BF16)
    acc = None
    for j in range(n_chunks):
        g = _dot(u, wi_ref[:, j * tf:(j + 1) * tf])
        up = _dot(u, wi_ref[:, F + j * tf:F + (j + 1) * tf])
        a = (g * _sigmoid(g) * up).astype(BF16)
        part = _dot(a, wo_ref[j * tf:(j + 1) * tf, :])
        acc = part if acc is None else acc + part
    out = x + (0.5 * gate) * acc
    if final_norm:
        out = _rms(out) * fnw_ref[...]
    o_ref[...] = out


def _ffn(x2, mods, mod_index, nw, wi, wo, fnw, *, sub, tm, final_norm):
    rows, D = x2.shape
    F = wo.shape[0]
    kern = functools.partial(_ffn_kernel, sub=sub, n_chunks=2, final_norm=final_norm)
    return pl.pallas_call(
        kern,
        out_shape=jax.ShapeDtypeStruct((rows, D), F32),
        grid=(rows // tm,),
        in_specs=[pl.BlockSpec((tm, D), lambda i: (i, 0)),
                  pl.BlockSpec((None, N_MOD, D), lambda i: (mod_index(i), 0, 0)),
                  _const_spec((1, D)),
                  _const_spec((D, 2 * F)),
                  _const_spec((F, D)),
                  _const_spec((1, D))],
        out_specs=pl.BlockSpec((tm, D), lambda i: (i, 0)),
        compiler_params=_cparams("parallel"),
        name="ffn_sub%d" % sub,
    )(x2, mods, nw.reshape(1, D), wi, wo, fnw.reshape(1, D))


def _chunk_prefix_sum(x, pos):
    n = x.shape[0]
    s = 1
    while s < GLA_CHUNK:
        x = x + jnp.where(pos >= s, pltpu.roll(x, s, 0), 0.0)
        s *= 2
    del n
    return x


def _mix_in_kernel(x_ref, mod_ref, nw_ref, wqkv_ref, wfd_ref, fup_ref, fb_ref, wxl_ref,
                   cw_ref, cb_ref, wg_ref, bg_ref, lam_ref, wrest_ref, *out_refs, seg, need_out):
    if need_out:
        (qdf_ref, kif_ref, kef_ref, qdb_ref, kib_ref, keb_ref, v_ref, decf_ref, decb_ref,
         af_ref, uf_ref, ab_ref, ub_ref, sg_ref, gy_ref, sa_ref, sb_ref) = out_refs
    else:
        (kef_ref, keb_ref, v_ref, decf_ref, decb_ref, af_ref, uf_ref, ab_ref, ub_ref) = out_refs
    tm = x_ref.shape[0]
    n_chunks = tm // GLA_CHUNK
    x = x_ref[...]
    shift = mod_ref[3:4, :]
    scale = mod_ref[4:5, :]
    u = (_rms(x) * nw_ref[...] * (1.0 + scale) + shift).astype(BF16)

    qkv = _dot(u, wqkv_ref[...])
    v_ref[...] = qkv[:, 2 * GLA_KEY:].astype(BF16)
    k = qkv[:, GLA_KEY:2 * GLA_KEY]
    fd = _dot(u, wfd_ref[...]).astype(BF16)
    logits = _dot(fd, fup_ref[...]) + fb_ref[...]
    la = (jnp.minimum(logits, 0.0) - jnp.log1p(jnp.exp(-jnp.abs(logits)))) * (1.0 / GLA_TAU)
    row = lax.broadcasted_iota(jnp.int32, (tm, 1), 0)
    cpos = row & (GLA_CHUNK - 1)
    csum = _chunk_prefix_sum(la, cpos)
    b_f = csum[:, :GLA_KEY]
    la_b = la[:, GLA_KEY:]
    pre_b = csum[:, GLA_KEY:] - la_b
    e_f_parts, c_b_parts = [], []
    for c in range(n_chunks):
        lo, hi = c * GLA_CHUNK, (c + 1) * GLA_CHUNK
        tot_f = b_f[hi - 1:hi, :]
        tot_b = csum[hi - 1:hi, GLA_KEY:]
        e_f_parts.append(tot_f - b_f[lo:hi, :])
        c_b_parts.append(tot_b - pre_b[lo:hi, :])
        decf_ref[c * SUBLANE:(c + 1) * SUBLANE, :] = jnp.broadcast_to(jnp.exp(tot_f), (SUBLANE, GLA_KEY))
        decb_ref[c * SUBLANE:(c + 1) * SUBLANE, :] = jnp.broadcast_to(jnp.exp(tot_b), (SUBLANE, GLA_KEY))
    e_f = jnp.concatenate(e_f_parts, axis=0)
    c_b = jnp.concatenate(c_b_parts, axis=0)
    kef_ref[...] = (k * jnp.exp(e_f)).astype(BF16)
    keb_ref[...] = (k * jnp.exp(pre_b)).astype(BF16)
    if need_out:
        q = qkv[:, :GLA_KEY] * (GLA_DK ** -0.5)
        qdf_ref[...] = (q * jnp.exp(b_f)).astype(BF16)
        kif_ref[...] = (k * jnp.exp(-b_f)).astype(BF16)
        qdb_ref[...] = (q * jnp.exp(c_b)).astype(BF16)
        kib_ref[...] = (k * jnp.exp(-c_b)).astype(BF16)

    xl = _dot(u, wxl_ref[...])
    assert seg == tm or (tm % seg == 0 and seg & (seg - 1) == 0)
    spos = row if seg == tm else row & (seg - 1)
    xc = cb_ref[...]
    for j in range(CONV_K):
        off = j - CONV_K // 2
        if off == 0:
            term = xl
        else:
            ok = (spos + off >= 0) & (spos + off < seg)
            term = jnp.where(ok, pltpu.roll(xl, (-off) % tm, 0), 0.0)
        xc = xc + term * cw_ref[j:j + 1, :]
    xcb = xc.astype(BF16)
    W = xc.shape[1]
    bw = W // LRU_BLOCKS
    sp = _softplus(-lam_ref[...])
    for n in range(LRU_BLOCKS):
        cs = slice(n * bw, (n + 1) * bw)
        z = _dot(xcb[:, cs], wg_ref[n])
        xcn = xc[:, cs]
        for d, (a_ref, u_ref) in enumerate(((af_ref, uf_ref), (ab_ref, ub_ref))):
            r = _sigmoid(z[:, (2 * d) * bw:(2 * d + 1) * bw] + bg_ref[2 * d:2 * d + 1, cs])
            i = _sigmoid(z[:, (2 * d + 1) * bw:(2 * d + 2) * bw] + bg_ref[2 * d + 1:2 * d + 2, cs])
            log_a = (-LRU_C) * r * sp[d:d + 1, cs]
            a = jnp.exp(log_a)
            a_ref[:, cs] = a
            u_ref[:, cs] = jnp.sqrt(jnp.tanh(-log_a) * (1.0 + a * a)) * (i * xcn)

    if need_out:
        rest = _dot(u, wrest_ref[...])
        g = rest[:, :GLA_VAL]
        sg_ref[...] = (g * _sigmoid(g)).astype(BF16)
        o1 = GLA_VAL
        gy_ref[...] = _gelu_tanh(rest[:, o1:o1 + W]).astype(BF16)
        D = x.shape[1]
        sa_ref[...] = _sigmoid(rest[:, o1 + W:o1 + W + D]).astype(BF16)
        sb_ref[...] = _sigmoid(rest[:, o1 + W + D:o1 + W + 2 * D]).astype(BF16)


def _mix_in(x2, mods, mod_index, nw, wts, *, B, L, tm, seg, need_out):
    rows, D = x2.shape
    W = wts["wxl"].shape[1]
    nt = L // tm
    grid = (B, nt)

    def tok(width):
        return pl.BlockSpec((tm, width), lambda b, j: (b * nt + j, 0))

    dec_spec = pl.BlockSpec((tm // SUBLANE, GLA_KEY), lambda b, j: (b * nt + j, 0))
    lru_spec = pl.BlockSpec((tm, W), lambda b, j: (j, b))
    tokbf = lambda width: jax.ShapeDtypeStruct((rows, width), BF16)
    dec_shape = jax.ShapeDtypeStruct((rows // SUBLANE, GLA_KEY), F32)
    lru_shape = jax.ShapeDtypeStruct((L, B * W), F32)

    out_shape, out_specs = [], []
    if need_out:
        out_shape += [tokbf(GLA_KEY)] * 6
        out_specs += [tok(GLA_KEY)] * 6
    else:
        out_shape += [tokbf(GLA_KEY)] * 2
        out_specs += [tok(GLA_KEY)] * 2
    out_shape += [tokbf(GLA_VAL), dec_shape, dec_shape] + [lru_shape] * 4
    out_specs += [tok(GLA_VAL), dec_spec, dec_spec] + [lru_spec] * 4
    if need_out:
        out_shape += [tokbf(GLA_VAL), tokbf(W), tokbf(D), tokbf(D)]
        out_specs += [tok(GLA_VAL), tok(W), tok(D), tok(D)]

    names = ["wqkv", "wfd", "fup", "fb", "wxl", "cw", "cb", "wg", "bg", "lam", "wrest"]
    w_list = [wts[n] for n in names]
    in_specs = [pl.BlockSpec((tm, D), lambda b, j: (b * nt + j, 0)),
                pl.BlockSpec((None, N_MOD, D), lambda b, j: (mod_index(b), 0, 0)),
                _const_spec((1, D))] + [_const_spec(w.shape) for w in w_list]
    kern = functools.partial(_mix_in_kernel, seg=seg, need_out=need_out)
    return pl.pallas_call(
        kern,
        out_shape=out_shape,
        grid=grid,
        in_specs=in_specs,
        out_specs=out_specs,
        compiler_params=_cparams("parallel", "parallel"),
        name="mix_in_out" if need_out else "mix_in_ctx",
    )(x2, mods, nw.reshape(1, D), *w_list)


def _lru_scan_kernel(af_ref, uf_ref, ab_ref, ub_ref, hf0_ref, hb0_ref, *refs, B, tt, need_out):
    if need_out:
        hf_ref, hb_ref, hfl_ref, hbl_ref, sf_ref, sb_ref = refs
    else:
        hfl_ref, hbl_ref, sf_ref, sb_ref = refs
    i = pl.program_id(1)

    @pl.when(i == 0)
    def _():
        sf_ref[...] = hf0_ref[...]
        sb_ref[...] = hb0_ref[...]

    def body(s, carry):
        hf, hb = carry
        rf = pl.multiple_of(s * B, B)
        rb = pl.multiple_of((tt - 1 - s) * B, B)
        hf = af_ref[pl.ds(rf, B), :] * hf + uf_ref[pl.ds(rf, B), :]
        hb = ab_ref[pl.ds(rb, B), :] * hb + ub_ref[pl.ds(rb, B), :]
        if need_out:
            hf_ref[pl.ds(rf, B), :] = hf
            hb_ref[pl.ds(rb, B), :] = hb
        return hf, hb

    hf, hb = lax.fori_loop(0, tt, body, (sf_ref[...], sb_ref[...]), unroll=2)
    sf_ref[...] = hf
    sb_ref[...] = hb
    hfl_ref[...] = hf
    hbl_ref[...] = hb


def _lru_scan(a_f, u_f, a_b, u_b, hf0, hb0, *, B, need_out):
    rows, W = a_f.shape
    L = rows // B
    tt = _pick_tile(L, 16, 1)
    wb = _pick_tile(W, 512, LANE)
    nblk = L // tt
    fwd = pl.BlockSpec((tt * B, wb), lambda w, i: (i, w))
    bwd = pl.BlockSpec((tt * B, wb), lambda w, i: (nblk - 1 - i, w))
    st = pl.BlockSpec((B, wb), lambda w, i: (0, w))
    out_shape = [jax.ShapeDtypeStruct((B, W), F32)] * 2
    out_specs = [st, st]
    if need_out:
        out_shape = [jax.ShapeDtypeStruct((rows, W), F32)] * 2 + out_shape
        out_specs = [fwd, bwd] + out_specs
    kern = functools.partial(_lru_scan_kernel, B=B, tt=tt, need_out=need_out)
    return pl.pallas_call(
        kern,
        out_shape=out_shape,
        grid=(W // wb, nblk),
        in_specs=[fwd, fwd, bwd, bwd, st, st],
        out_specs=out_specs,
        scratch_shapes=[pltpu.VMEM((B, wb), F32), pltpu.VMEM((B, wb), F32)],
        compiler_params=_cparams("parallel", "arbitrary"),
        name="lru_scan_out" if need_out else "lru_scan_ctx",
    )(a_f, u_f, a_b, u_b, hf0, hb0)


def _dot_tn(a, b):
    return lax.dot_general(a, b, (((0,), (0,)), ((), ())), preferred_element_type=F32)


def _dot_nt(a, b):
    return lax.dot_general(a, b, (((1,), (1,)), ((), ())), preferred_element_type=F32)


def _gla_kernel(ckef_ref, ckeb_ref, cv_ref, cdf_ref, cdb_ref,
                qdf_ref, kif_ref, kef_ref, qdb_ref, kib_ref, keb_ref, v_ref, df_ref, db_ref,
                sg_ref, gnw_ref, o_ref, of_ref):
    C = GLA_CHUNK
    nc = cv_ref.shape[0] // C
    n = v_ref.shape[0] // C
    ri = lax.broadcasted_iota(jnp.int32, (C, C), 0)
    ci = lax.broadcasted_iota(jnp.int32, (C, C), 1)
    causal = ci <= ri
    anti_strict = ci > ri

    def rows(c):
        return pl.ds(pl.multiple_of(c * C, C), C)

    def dec_rows(c):
        return pl.ds(pl.multiple_of(c * SUBLANE, SUBLANE), SUBLANE)

    def advance(st, ke, v, dec8):
        return st * dec8[0:1, :] + _dot_tn(v, ke)

    def attend(st, qd, ki, v, mask):
        sc = jnp.where(mask, _dot_nt(qd, ki), 0.0).astype(BF16)
        return _dot(sc, v) + _dot_nt(qd, st.astype(BF16))

    st0 = jnp.zeros((GLA_DV, GLA_DK), F32)

    def ctx_f(c, st):
        return advance(st, ckef_ref[rows(c), :], cv_ref[rows(c), :], cdf_ref[dec_rows(c), :])

    def lat_f(c, st):
        r = rows(c)
        v = v_ref[r, :]
        of_ref[r, :] = attend(st, qdf_ref[r, :], kif_ref[r, :], v, causal)
        return advance(st, kef_ref[r, :], v, df_ref[dec_rows(c), :])

    st = lax.fori_loop(0, nc, ctx_f, st0)
    lax.fori_loop(0, n, lat_f, st)

    def ctx_b(s, st):
        c = nc - 1 - s
        return advance(st, ckeb_ref[rows(c), :], cv_ref[rows(c), :], cdb_ref[dec_rows(c), :])

    def lat_b(s, st):
        c = n - 1 - s
        r = rows(c)
        v = v_ref[r, :]
        o = of_ref[r, :] + attend(st, qdb_ref[r, :], kib_ref[r, :], v, anti_strict)
        o_ref[r, :] = (_rms(o) * gnw_ref[...] * sg_ref[r, :].astype(F32)).astype(BF16)
        return advance(st, keb_ref[r, :], v, db_ref[dec_rows(c), :])

    st = lax.fori_loop(0, nc, ctx_b, st0)
    lax.fori_loop(0, n, lat_b, st)


def _gla(ctx_t, lat_t, gnw, *, B, Tc, T):
    ckef, ckeb, cv, cdf, cdb = ctx_t
    qdf, kif, kef, qdb, kib, keb, v, df, db, sg = lat_t
    H = GLA_HEADS

    def key(L):
        return pl.BlockSpec((L, GLA_DK), lambda b, h: (b, h))

    def val(L):
        return pl.BlockSpec((L, GLA_DV), lambda b, h: (b, h))

    def dec(L):
        return pl.BlockSpec((L // SUBLANE, GLA_DK), lambda b, h: (b, h))

    return pl.pallas_call(
        _gla_kernel,
        out_shape=jax.ShapeDtypeStruct((B * T, GLA_VAL), BF16),
        grid=(B, H),
        in_specs=[key(Tc), key(Tc), val(Tc), dec(Tc), dec(Tc),
                  key(T), key(T), key(T), key(T), key(T), key(T), val(T), dec(T), dec(T),
                  val(T), _const_spec((1, GLA_DV))],
        out_specs=val(T),
        scratch_shapes=[pltpu.VMEM((T, GLA_DV), F32)],
        compiler_params=_cparams("parallel", "parallel"),
        name="gla",
    )(ckef, ckeb, cv, cdf, cdb, qdf, kif, kef, qdb, kib, keb, v, df, db, sg, gnw.reshape(1, GLA_DV))


def _mix_out_kernel(x_ref, mod_ref, og_ref, hf_ref, hb_ref, gy_ref, sa_ref, sb_ref,
                    wgla_ref, wlru_ref, wo_ref, o_ref):
    y_gla = _dot(og_ref[...], wgla_ref[...])
    hl = ((hf_ref[...] + hb_ref[...]) * gy_ref[...].astype(F32)).astype(BF16)
    y_lru = _dot(hl, wlru_ref[...])
    merged = sa_ref[...].astype(F32) * y_gla + sb_ref[...].astype(F32) * y_lru
    y = _dot(merged.astype(BF16), wo_ref[...])
    o_ref[...] = x_ref[...] + mod_ref[5:6, :] * y


def _mix_out(x2, mods, og, hf, hb, gy, sa, sb, wgla, wlru, wo, *, B, T, tm):
    rows, D = x2.shape
    W = wlru.shape[0]
    nt = T // tm
    tok = lambda width: pl.BlockSpec((tm, width), lambda b, j: (b * nt + j, 0))
    lru = pl.BlockSpec((tm, W), lambda b, j: (j, b))
    return pl.pallas_call(
        _mix_out_kernel,
        out_shape=jax.ShapeDtypeStruct((rows, D), F32),
        grid=(B, nt),
        in_specs=[tok(D), pl.BlockSpec((None, N_MOD, D), lambda b, j: (b, 0, 0)),
                  tok(GLA_VAL), lru, lru, tok(W), tok(D), tok(D),
                  _const_spec(wgla.shape), _const_spec(wlru.shape), _const_spec(wo.shape)],
        out_specs=tok(D),
        compiler_params=_cparams("parallel", "parallel"),
        name="mix_out",
    )(x2, mods, og, hf, hb, gy, sa, sb, wgla, wlru, wo)


def _prep_mixer_weights(w_in, fup, fb, conv_w, conv_b, wr, br, wi, bi, lam):
    D = w_in.shape[0]
    o = 0
    wqkv = w_in[:, o:o + 2 * GLA_KEY + GLA_VAL]
    o += 2 * GLA_KEY + GLA_VAL
    w_g = w_in[:, o:o + GLA_VAL]
    o += GLA_VAL
    wfd = w_in[:, o:o + 2 * GLA_RANK]
    o += 2 * GLA_RANK
    W = lam.shape[1]
    wxl = w_in[:, o:o + W]
    o += W
    w_tail = w_in[:, o:]
    wfd_p = jnp.zeros((D, LANE), F32).at[:, :2 * GLA_RANK].set(wfd)
    fup_p = jnp.zeros((LANE, 2 * GLA_KEY), F32)
    fup_p = fup_p.at[:GLA_RANK, :GLA_KEY].set(fup[0]).at[GLA_RANK:2 * GLA_RANK, GLA_KEY:].set(fup[1])
    wg = jnp.concatenate([wr[0], wi[0], wr[1], wi[1]], axis=-1)
    return {
        "wqkv": wqkv.astype(BF16),
        "wfd": wfd_p.astype(BF16),
        "fup": fup_p.astype(BF16),
        "fb": jnp.concatenate([fb[0], fb[1]]).reshape(1, 2 * GLA_KEY),
        "wxl": wxl.astype(BF16),
        "cw": conv_w,
        "cb": conv_b.reshape(1, W),
        "wg": wg.astype(BF16),
        "bg": jnp.stack([br[0], bi[0], br[1], bi[1]]),
        "lam": lam,
        "wrest": jnp.concatenate([w_g, w_tail], axis=1).astype(BF16),
    }


def kernel(x, c, ctx, c_ctx, w_ada, b_ada, norm_w, ffn1_wi, ffn1_wo, ffn2_wi, ffn2_wo, w_in, gla_fup, gla_fb, gla_norm_w, conv_w, conv_b, lru_wr, lru_br, lru_wi, lru_bi, lru_lam, w_out_gla, w_out_lru, w_o, final_norm_w):
    B, T, D = x.shape
    Tc = ctx.shape[1]
    depth = w_ada.shape[0]
    assert depth == 1, "single trunk layer"
    assert B % SUBLANE == 0 and T % GRID_W == 0 and Tc % GLA_CHUNK == 0
    l = 0

    R = -(-(B + 1) // SUBLANE) * SUBLANE
    cs = jnp.zeros((R, D), F32).at[:B].set(c).at[B].set(c_ctx)
    mods = _ada(cs, w_ada[l], b_ada[l]).reshape(R, N_MOD, D)

    x2 = x.reshape(B * T, D)
    c2 = ctx.reshape(B * Tc, D)
    tm_lat = _pick_tile(T, 512, GRID_W)
    wi1, wo1 = ffn1_wi[l].astype(BF16), ffn1_wo[l].astype(BF16)
    wi2, wo2 = ffn2_wi[l].astype(BF16), ffn2_wo[l].astype(BF16)
    nt_lat = T // tm_lat
    tm_cf = _pick_tile(B * Tc, 512, SUBLANE)

    h = _ffn(x2, mods, lambda i: i // nt_lat, norm_w[l, 0], wi1, wo1, final_norm_w,
             sub=0, tm=tm_lat, final_norm=False)
    hc = _ffn(c2, mods, lambda i: B, norm_w[l, 0], wi1, wo1, final_norm_w,
              sub=0, tm=tm_cf, final_norm=False)

    wts = _prep_mixer_weights(w_in[l], gla_fup[l], gla_fb[l], conv_w[l], conv_b[l],
                              lru_wr[l], lru_br[l], lru_wi[l], lru_bi[l], lru_lam[l])
    W = lru_lam.shape[-1]
    ctx_o = _mix_in(hc, mods, lambda b: B, norm_w[l, 1], wts,
                    B=B, L=Tc, tm=Tc, seg=Tc, need_out=False)
    ckef, ckeb, cv, cdf, cdb, caf, cuf, cab, cub = ctx_o
    lat_o = _mix_in(h, mods, lambda b: b, norm_w[l, 1], wts,
                    B=B, L=T, tm=tm_lat, seg=GRID_W, need_out=True)
    (qdf, kif, kef, qdb, kib, keb, v, df, db, af, uf, ab, ub, sg, gy, sa, sb) = lat_o

    zero = jnp.zeros((B, W), F32)
    as_tb = lambda t, L: t.reshape(L * B, W)
    hf_c, hb_c = _lru_scan(as_tb(caf, Tc), as_tb(cuf, Tc), as_tb(cab, Tc), as_tb(cub, Tc),
                           zero, zero, B=B, need_out=False)
    hf, hb, _, _ = _lru_scan(as_tb(af, T), as_tb(uf, T), as_tb(ab, T), as_tb(ub, T),
                             hf_c, hb_c, B=B, need_out=True)

    og = _gla((ckef, ckeb, cv, cdf, cdb), (qdf, kif, kef, qdb, kib, keb, v, df, db, sg),
              gla_norm_w[l], B=B, Tc=Tc, T=T)

    h = _mix_out(h, mods, og, hf.reshape(T, B * W), hb.reshape(T, B * W), gy, sa, sb,
                 w_out_gla[l].astype(BF16), w_out_lru[l].astype(BF16), w_o[l].astype(BF16),
                 B=B, T=T, tm=tm_lat)

    out = _ffn(h, mods, lambda i: i // nt_lat, norm_w[l, 2], wi2, wo2, final_norm_w,
               sub=2, tm=tm_lat, final_norm=True)
    return out.reshape(B, T, D)
```

```python
import functools

import jax
import jax.numpy as jnp
from jax import lax
from jax.experimental import pallas as pl
from jax.experimental.pallas import tpu as pltpu

F32 = jnp.float32
BF16 = jnp.bfloat16

N_MOD = 9
GLA_HEADS = 4
GLA_DK = 128
GLA_DV = 256
GLA_KEY = GLA_HEADS * GLA_DK
GLA_VAL = GLA_HEADS * GLA_DV
GLA_RANK = 16
GLA_TAU = 16.0
GLA_CHUNK = 64
GRID_W = 64
LRU_BLOCKS = 8
LRU_C = 8.0
CONV_K = 4
EPS = 1e-6

LANE = 128
SUBLANE = 8
VMEM_LIMIT = 56 * 1024 * 1024


def _cparams(*sem):
    return pltpu.CompilerParams(dimension_semantics=sem, vmem_limit_bytes=VMEM_LIMIT)


def _const_spec(shape):
    nd = len(shape)
    return pl.BlockSpec(shape, lambda *_: (0,) * nd, pipeline_mode=pl.Buffered(1))


def _dot(a, b):
    return jnp.dot(a, b, preferred_element_type=F32)


def _sigmoid(x):
    return 1.0 / (1.0 + jnp.exp(-x))


def _softplus(x):
    return jnp.maximum(x, 0.0) + jnp.log1p(jnp.exp(-jnp.abs(x)))


def _gelu_tanh(x):
    c = 0.7978845608028654
    return 0.5 * x * (1.0 + jnp.tanh(c * (x + 0.044715 * (x * x * x))))


def _rms(x):
    return x * lax.rsqrt(jnp.mean(x * x, axis=-1, keepdims=True) + EPS)


def _pick_tile(n, target, quantum):
    best = None
    t = quantum
    while t <= min(n, target):
        if n % t == 0:
            best = t
        t += quantum
    assert best is not None, (n, target, quantum)
    return best


def _ada_kernel(c_ref, w_ref, b_ref, o_ref):
    c = c_ref[...]
    s = (c * _sigmoid(c)).astype(BF16)
    o_ref[...] = _dot(s, w_ref[...].astype(BF16)) + b_ref[...]


def _ada(cs, w_ada, b_ada):
    R, D = cs.shape
    N = w_ada.shape[1]
    tn = _pick_tile(N, 2304, LANE)
    return pl.pallas_call(
        _ada_kernel,
        out_shape=jax.ShapeDtypeStruct((R, N), F32),
        grid=(N // tn,),
        in_specs=[pl.BlockSpec((R, D), lambda j: (0, 0)),
                  pl.BlockSpec((D, tn), lambda j: (0, j)),
                  pl.BlockSpec((1, tn), lambda j: (0, j))],
        out_specs=pl.BlockSpec((R, tn), lambda j: (0, j)),
        compiler_params=_cparams("parallel"),
        name="ada",
    )(cs, w_ada, b_ada.reshape(1, N))


def _ffn_kernel(x_ref, mod_ref, nw_ref, wi_ref, wo_ref, fnw_ref, o_ref, *, sub, n_chunks, final_norm):
    F = wo_ref.shape[0]
    tf = F // n_chunks
    x = x_ref[...]
    shift = mod_ref[3 * sub:3 * sub + 1, :]
    scale = mod_ref[3 * sub + 1:3 * sub + 2, :]
    gate = mod_ref[3 * sub + 2:3 * sub + 3, :]
    u = (_rms(x) * nw_ref[...] * (1.0 + scale) + shift).astype(BF16)
    acc = None
    for j in range(n_chunks):
        g = _dot(u, wi_ref[:, j * tf:(j + 1) * tf])
        up = _dot(u, wi_ref[:, F + j * tf:F + (j + 1) * tf])
        a = (g * _sigmoid(g) * up).astype(BF16)
        part = _dot(a, wo_ref[j * tf:(j + 1) * tf, :])
        acc = part if acc is None else acc + part
    out = x + (0.5 * gate) * acc
    if final_norm:
        out = _rms(out) * fnw_ref[...]
    o_ref[...] = out


def _ffn(x2, mods, mod_index, nw, wi, wo, fnw, *, sub, tm, final_norm):
    rows, D = x2.shape
    F = wo.shape[0]
    kern = functools.partial(_ffn_kernel, sub=sub, n_chunks=2, final_norm=final_norm)
    return pl.pallas_call(
        kern,
        out_shape=jax.ShapeDtypeStruct((rows, D), F32),
        grid=(rows // tm,),
        in_specs=[pl.BlockSpec((tm, D), lambda i: (i, 0)),
                  pl.BlockSpec((None, N_MOD, D), lambda i: (mod_index(i), 0, 0)),
                  _const_spec((1, D)),
                  _const_spec((D, 2 * F)),
                  _const_spec((F, D)),
                  _const_spec((1, D))],
        out_specs=pl.BlockSpec((tm, D), lambda i: (i, 0)),
        compiler_params=_cparams("parallel"),
        name="ffn_sub%d" % sub,
    )(x2, mods, nw.reshape(1, D), wi, wo, fnw.reshape(1, D))


def _chunk_prefix_sum(x, pos):
    n = x.shape[0]
    s = 1
    while s < GLA_CHUNK:
        x = x + jnp.where(pos >= s, pltpu.roll(x, s, 0), 0.0)
        s *= 2
    del n
    return x


def _mix_in_kernel(x_ref, mod_ref, nw_ref, wqkv_ref, wfd_ref, fup_ref, fb_ref, wxl_ref,
                   cw_ref, cb_ref, wg_ref, bg_ref, lam_ref, wrest_ref, *out_refs, seg, need_out):
    if need_out:
        (qdf_ref, kif_ref, kef_ref, qdb_ref, kib_ref, keb_ref, v_ref, decf_ref, decb_ref,
         af_ref, uf_ref, ab_ref, ub_ref, sg_ref, gy_ref, sa_ref, sb_ref) = out_refs
    else:
        (kef_ref, keb_ref, v_ref, decf_ref, decb_ref, af_ref, uf_ref, ab_ref, ub_ref) = out_refs
    tm = x_ref.shape[0]
    n_chunks = tm // GLA_CHUNK
    x = x_ref[...]
    shift = mod_ref[3:4, :]
    scale = mod_ref[4:5, :]
    u = (_rms(x) * nw_ref[...] * (1.0 + scale) + shift).astype(BF16)

    qkv = _dot(u, wqkv_ref[...])
    v_ref[...] = qkv[:, 2 * GLA_KEY:].astype(BF16)
    k = qkv[:, GLA_KEY:2 * GLA_KEY]
    fd = _dot(u, wfd_ref[...]).astype(BF16)
    logits = _dot(fd, fup_ref[...]) + fb_ref[...]
    la = (jnp.minimum(logits, 0.0) - jnp.log1p(jnp.exp(-jnp.abs(logits)))) * (1.0 / GLA_TAU)
    row = lax.broadcasted_iota(jnp.int32, (tm, 1), 0)
    cpos = row & (GLA_CHUNK - 1)
    csum = _chunk_prefix_sum(la, cpos)
    b_f = csum[:, :GLA_KEY]
    la_b = la[:, GLA_KEY:]
    pre_b = csum[:, GLA_KEY:] - la_b
    e_f_parts, c_b_parts = [], []
    for c in range(n_chunks):
        lo, hi = c * GLA_CHUNK, (c + 1) * GLA_CHUNK
        tot_f = b_f[hi - 1:hi, :]
        tot_b = csum[hi - 1:hi, GLA_KEY:]
        e_f_parts.append(tot_f - b_f[lo:hi, :])
        c_b_parts.append(tot_b - pre_b[lo:hi, :])
        decf_ref[c * SUBLANE:(c + 1) * SUBLANE, :] = jnp.broadcast_to(jnp.exp(tot_f), (SUBLANE, GLA_KEY))
        decb_ref[c * SUBLANE:(c + 1) * SUBLANE, :] = jnp.broadcast_to(jnp.exp(tot_b), (SUBLANE, GLA_KEY))
    e_f = jnp.concatenate(e_f_parts, axis=0)
    c_b = jnp.concatenate(c_b_parts, axis=0)
    kef_ref[...] = (k * jnp.exp(e_f)).astype(BF16)
    keb_ref[...] = (k * jnp.exp(pre_b)).astype(BF16)
    if need_out:
        q = qkv[:, :GLA_KEY] * (GLA_DK ** -0.5)
        qdf_ref[...] = (q * jnp.exp(b_f)).astype(BF16)
        kif_ref[...] = (k * jnp.exp(-b_f)).astype(BF16)
        qdb_ref[...] = (q * jnp.exp(c_b)).astype(BF16)
        kib_ref[...] = (k * jnp.exp(-c_b)).astype(BF16)

    xl = _dot(u, wxl_ref[...])
    assert seg == tm or (tm % seg == 0 and seg & (seg - 1) == 0)
    spos = row if seg == tm else row & (seg - 1)
    xc = cb_ref[...]
    for j in range(CONV_K):
        off = j - CONV_K // 2
        if off == 0:
            term = xl
        else:
            ok = (spos + off >= 0) & (spos + off < seg)
            term = jnp.where(ok, pltpu.roll(xl, (-off) % tm, 0), 0.0)
        xc = xc + term * cw_ref[j:j + 1, :]
    xcb = xc.astype(BF16)
    W = xc.shape[1]
    bw = W // LRU_BLOCKS
    sp = _softplus(-lam_ref[...])
    for n in range(LRU_BLOCKS):
        cs = slice(n * bw, (n + 1) * bw)
        z = _dot(xcb[:, cs], wg_ref[n])
        xcn = xc[:, cs]
        for d, (a_ref, u_ref) in enumerate(((af_ref, uf_ref), (ab_ref, ub_ref))):
            r = _sigmoid(z[:, (2 * d) * bw:(2 * d + 1) * bw] + bg_ref[2 * d:2 * d + 1, cs])
            i = _sigmoid(z[:, (2 * d + 1) * bw:(2 * d + 2) * bw] + bg_ref[2 * d + 1:2 * d + 2, cs])
            log_a = (-LRU_C) * r * sp[d:d + 1, cs]
            a = jnp.exp(log_a)
            a_ref[:, :, cs] = a.reshape(tm // SUBLANE, SUBLANE, bw)
            uu = jnp.sqrt(jnp.tanh(-log_a) * (1.0 + a * a)) * (i * xcn)
            u_ref[:, :, cs] = uu.reshape(tm // SUBLANE, SUBLANE, bw)

    if need_out:
        rest = _dot(u, wrest_ref[...])
        g = rest[:, :GLA_VAL]
        sg_ref[...] = (g * _sigmoid(g)).astype(BF16)
        o1 = GLA_VAL
        gy_ref[...] = _gelu_tanh(rest[:, o1:o1 + W]).astype(BF16)
        D = x.shape[1]
        sa_ref[...] = _sigmoid(rest[:, o1 + W:o1 + W + D]).astype(BF16)
        sb_ref[...] = _sigmoid(rest[:, o1 + W + D:o1 + W + 2 * D]).astype(BF16)


def _mix_in(x2, mods, mod_index, nw, wts, *, B, L, tm, seg, need_out):
    rows, D = x2.shape
    W = wts["wxl"].shape[1]
    nt = L // tm
    grid = (B, nt)

    def tok(width):
        return pl.BlockSpec((tm, width), lambda b, j: (b * nt + j, 0))

    dec_spec = pl.BlockSpec((tm // SUBLANE, GLA_KEY), lambda b, j: (b * nt + j, 0))
    lru_spec = pl.BlockSpec((tm // SUBLANE, None, SUBLANE, W), lambda b, j: (j, b, 0, 0))
    tokbf = lambda width: jax.ShapeDtypeStruct((rows, width), BF16)
    dec_shape = jax.ShapeDtypeStruct((rows // SUBLANE, GLA_KEY), F32)
    lru_shape = jax.ShapeDtypeStruct((L // SUBLANE, B, SUBLANE, W), F32)

    out_shape, out_specs = [], []
    if need_out:
        out_shape += [tokbf(GLA_KEY)] * 6
        out_specs += [tok(GLA_KEY)] * 6
    else:
        out_shape += [tokbf(GLA_KEY)] * 2
        out_specs += [tok(GLA_KEY)] * 2
    out_shape += [tokbf(GLA_VAL), dec_shape, dec_shape] + [lru_shape] * 4
    out_specs += [tok(GLA_VAL), dec_spec, dec_spec] + [lru_spec] * 4
    if need_out:
        out_shape += [tokbf(GLA_VAL), tokbf(W), tokbf(D), tokbf(D)]
        out_specs += [tok(GLA_VAL), tok(W), tok(D), tok(D)]

    names = ["wqkv", "wfd", "fup", "fb", "wxl", "cw", "cb", "wg", "bg", "lam", "wrest"]
    w_list = [wts[n] for n in names]
    in_specs = [pl.BlockSpec((tm, D), lambda b, j: (b * nt + j, 0)),
                pl.BlockSpec((None, N_MOD, D), lambda b, j: (mod_index(b), 0, 0)),
                _const_spec((1, D))] + [_const_spec(w.shape) for w in w_list]
    kern = functools.partial(_mix_in_kernel, seg=seg, need_out=need_out)
    return pl.pallas_call(
        kern,
        out_shape=out_shape,
        grid=grid,
        in_specs=in_specs,
        out_specs=out_specs,
        compiler_params=_cparams("parallel", "parallel"),
        name="mix_in_out" if need_out else "mix_in_ctx",
    )(x2, mods, nw.reshape(1, D), *w_list)


def _lru_scan_kernel(af_ref, uf_ref, ab_ref, ub_ref, hf0_ref, hb0_ref, *refs, B, tt, need_out):
    if need_out:
        hf_ref, hb_ref, hfl_ref, hbl_ref, sf_ref, sb_ref = refs
    else:
        hfl_ref, hbl_ref, sf_ref, sb_ref = refs
    i = pl.program_id(1)

    @pl.when(i == 0)
    def _():
        sf_ref[...] = hf0_ref[...]
        sb_ref[...] = hb0_ref[...]

    def at(t):
        return pl.ds((t // SUBLANE) * (B * SUBLANE) + t % SUBLANE, B, stride=SUBLANE)

    def body(s, carry):
        hf, hb = carry
        rf = at(s)
        rb = at(tt - 1 - s)
        hf = af_ref[rf, :] * hf + uf_ref[rf, :]
        hb = ab_ref[rb, :] * hb + ub_ref[rb, :]
        if need_out:
            hf_ref[rf, :] = hf
            hb_ref[rb, :] = hb
        return hf, hb

    hf, hb = lax.fori_loop(0, tt, body, (sf_ref[...], sb_ref[...]), unroll=SUBLANE)
    sf_ref[...] = hf
    sb_ref[...] = hb
    hfl_ref[...] = hf
    hbl_ref[...] = hb


def _lru_scan(a_f, u_f, a_b, u_b, hf0, hb0, *, B, need_out):
    rows, W = a_f.shape
    L = rows // B
    tt = SUBLANE * _pick_tile(L // SUBLANE, 16, 1)
    wb = LANE
    nblk = L // tt
    fwd = pl.BlockSpec((tt * B, wb), lambda w, i: (i, w))
    bwd = pl.BlockSpec((tt * B, wb), lambda w, i: (nblk - 1 - i, w))
    st = pl.BlockSpec((B, wb), lambda w, i: (0, w))
    out_shape = [jax.ShapeDtypeStruct((B, W), F32)] * 2
    out_specs = [st, st]
    if need_out:
        out_shape = [jax.ShapeDtypeStruct((rows, W), F32)] * 2 + out_shape
        out_specs = [fwd, bwd] + out_specs
    kern = functools.partial(_lru_scan_kernel, B=B, tt=tt, need_out=need_out)
    return pl.pallas_call(
        kern,
        out_shape=out_shape,
        grid=(W // wb, nblk),
        in_specs=[fwd, fwd, bwd, bwd, st, st],
        out_specs=out_specs,
        scratch_shapes=[pltpu.VMEM((B, wb), F32), pltpu.VMEM((B, wb), F32)],
        compiler_params=_cparams("parallel", "arbitrary"),
        name="lru_scan_out" if need_out else "lru_scan_ctx",
    )(a_f, u_f, a_b, u_b, hf0, hb0)


def _dot_tn(a, b):
    return lax.dot_general(a, b, (((0,), (0,)), ((), ())), preferred_element_type=F32)


def _dot_nt(a, b):
    return lax.dot_general(a, b, (((1,), (1,)), ((), ())), preferred_element_type=F32)


def _gla_kernel(ckef_ref, ckeb_ref, cv_ref, cdf_ref, cdb_ref,
                qdf_ref, kif_ref, kef_ref, qdb_ref, kib_ref, keb_ref, v_ref, df_ref, db_ref,
                sg_ref, gnw_ref, o_ref, of_ref, ob_ref, *, hp):
    C = GLA_CHUNK
    nc = cv_ref.shape[0] // C
    T = v_ref.shape[0]
    n = T // C
    ri = lax.broadcasted_iota(jnp.int32, (C, C), 0)
    ci = lax.broadcasted_iota(jnp.int32, (C, C), 1)
    causal = ci <= ri
    anti_strict = ci > ri

    def rows(c):
        return pl.ds(pl.multiple_of(c * C, C), C)

    def dec_rows(c):
        return pl.ds(pl.multiple_of(c * SUBLANE, SUBLANE), SUBLANE)

    def advance(st, ke, v, dec8):
        return st * dec8[0:1, :] + _dot_tn(v, ke)

    def attend(st, qd, ki, v, mask):
        sc = jnp.where(mask, _dot_nt(qd, ki), 0.0).astype(BF16)
        return _dot(sc, v) + _dot_nt(qd, st.astype(BF16))

    def kc(h):
        return slice(h * GLA_DK, (h + 1) * GLA_DK)

    def vc(h):
        return slice(h * GLA_DV, (h + 1) * GLA_DV)

    def ctx_step(s, sts):
        rf, rb = rows(s), rows(nc - 1 - s)
        df8, db8 = cdf_ref[dec_rows(s), :], cdb_ref[dec_rows(nc - 1 - s), :]
        new = []
        for h in range(hp):
            new.append(advance(sts[2 * h], ckef_ref[rf, kc(h)], cv_ref[rf, vc(h)], df8[:, kc(h)]))
            new.append(advance(sts[2 * h + 1], ckeb_ref[rb, kc(h)], cv_ref[rb, vc(h)], db8[:, kc(h)]))
        return tuple(new)

    def lat_step(s, sts):
        rf, rb = rows(s), rows(n - 1 - s)
        df8, db8 = df_ref[dec_rows(s), :], db_ref[dec_rows(n - 1 - s), :]
        new = []
        for h in range(hp):
            vf = v_ref[rf, vc(h)]
            of_ref[rf, vc(h)] = attend(sts[2 * h], qdf_ref[rf, kc(h)], kif_ref[rf, kc(h)], vf, causal)
            new.append(advance(sts[2 * h], kef_ref[rf, kc(h)], vf, df8[:, kc(h)]))
            vb = v_ref[rb, vc(h)]
            ob_ref[rb, vc(h)] = attend(sts[2 * h + 1], qdb_ref[rb, kc(h)], kib_ref[rb, kc(h)], vb, anti_strict)
            new.append(advance(sts[2 * h + 1], keb_ref[rb, kc(h)], vb, db8[:, kc(h)]))
        return tuple(new)

    st0 = tuple(jnp.zeros((GLA_DV, GLA_DK), F32) for _ in range(2 * hp))
    sts = lax.fori_loop(0, nc, ctx_step, st0)
    lax.fori_loop(0, n, lat_step, sts, unroll=2)

    rb_ = _pick_tile(T, 256, C)

    def finish(i, carry):
        r = pl.ds(pl.multiple_of(i * rb_, rb_), rb_)
        for h in range(hp):
            o = of_ref[r, vc(h)] + ob_ref[r, vc(h)]
            o_ref[r, vc(h)] = (_rms(o) * gnw_ref[...] * sg_ref[r, vc(h)].astype(F32)).astype(BF16)
        return carry

    lax.fori_loop(0, T // rb_, finish, 0)


def _gla(ctx_t, lat_t, gnw, *, B, Tc, T):
    ckef, ckeb, cv, cdf, cdb = ctx_t
    qdf, kif, kef, qdb, kib, keb, v, df, db, sg = lat_t
    hp = 2
    assert GLA_HEADS % hp == 0

    def key(L):
        return pl.BlockSpec((L, hp * GLA_DK), lambda b, h: (b, h))

    def val(L):
        return pl.BlockSpec((L, hp * GLA_DV), lambda b, h: (b, h))

    def dec(L):
        return pl.BlockSpec((L // SUBLANE, hp * GLA_DK), lambda b, h: (b, h))

    return pl.pallas_call(
        functools.partial(_gla_kernel, hp=hp),
        out_shape=jax.ShapeDtypeStruct((B * T, GLA_VAL), BF16),
        grid=(B, GLA_HEADS // hp),
        in_specs=[key(Tc), key(Tc), val(Tc), dec(Tc), dec(Tc),
                  key(T), key(T), key(T), key(T), key(T), key(T), val(T), dec(T), dec(T),
                  val(T), _const_spec((1, GLA_DV))],
        out_specs=val(T),
        scratch_shapes=[pltpu.VMEM((T, hp * GLA_DV), F32), pltpu.VMEM((T, hp * GLA_DV), F32)],
        compiler_params=_cparams("parallel", "parallel"),
        name="gla",
    )(ckef, ckeb, cv, cdf, cdb, qdf, kif, kef, qdb, kib, keb, v, df, db, sg, gnw.reshape(1, GLA_DV))


def _mix_out_kernel(x_ref, mod_ref, og_ref, hf_ref, hb_ref, gy_ref, sa_ref, sb_ref,
                    wgla_ref, wlru_ref, wo_ref, o_ref):
    y_gla = _dot(og_ref[...], wgla_ref[...])
    tm, W = gy_ref.shape
    hsum = (hf_ref[...] + hb_ref[...]).reshape(tm, W)
    hl = (hsum * gy_ref[...].astype(F32)).astype(BF16)
    y_lru = _dot(hl, wlru_ref[...])
    merged = sa_ref[...].astype(F32) * y_gla + sb_ref[...].astype(F32) * y_lru
    y = _dot(merged.astype(BF16), wo_ref[...])
    o_ref[...] = x_ref[...] + mod_ref[5:6, :] * y


def _mix_out(x2, mods, og, hf, hb, gy, sa, sb, wgla, wlru, wo, *, B, T, tm):
    rows, D = x2.shape
    W = wlru.shape[0]
    nt = T // tm
    tok = lambda width: pl.BlockSpec((tm, width), lambda b, j: (b * nt + j, 0))
    lru = pl.BlockSpec((tm // SUBLANE, None, SUBLANE, W), lambda b, j: (j, b, 0, 0))
    return pl.pallas_call(
        _mix_out_kernel,
        out_shape=jax.ShapeDtypeStruct((rows, D), F32),
        grid=(B, nt),
        in_specs=[tok(D), pl.BlockSpec((None, N_MOD, D), lambda b, j: (b, 0, 0)),
                  tok(GLA_VAL), lru, lru, tok(W), tok(D), tok(D),
                  _const_spec(wgla.shape), _const_spec(wlru.shape), _const_spec(wo.shape)],
        out_specs=tok(D),
        compiler_params=_cparams("parallel", "parallel"),
        name="mix_out",
    )(x2, mods, og, hf, hb, gy, sa, sb, wgla, wlru, wo)


def _prep_mixer_weights(w_in, fup, fb, conv_w, conv_b, wr, br, wi, bi, lam):
    D = w_in.shape[0]
    o = 0
    wqkv = w_in[:, o:o + 2 * GLA_KEY + GLA_VAL]
    o += 2 * GLA_KEY + GLA_VAL
    w_g = w_in[:, o:o + GLA_VAL]
    o += GLA_VAL
    wfd = w_in[:, o:o + 2 * GLA_RANK]
    o += 2 * GLA_RANK
    W = lam.shape[1]
    wxl = w_in[:, o:o + W]
    o += W
    w_tail = w_in[:, o:]
    wfd_p = jnp.zeros((D, LANE), F32).at[:, :2 * GLA_RANK].set(wfd)
    fup_p = jnp.zeros((LANE, 2 * GLA_KEY), F32)
    fup_p = fup_p.at[:GLA_RANK, :GLA_KEY].set(fup[0]).at[GLA_RANK:2 * GLA_RANK, GLA_KEY:].set(fup[1])
    wg = jnp.concatenate([wr[0], wi[0], wr[1], wi[1]], axis=-1)
    return {
        "wqkv": wqkv.astype(BF16),
        "wfd": wfd_p.astype(BF16),
        "fup": fup_p.astype(BF16),
        "fb": jnp.concatenate([fb[0], fb[1]]).reshape(1, 2 * GLA_KEY),
        "wxl": wxl.astype(BF16),
        "cw": conv_w,
        "cb": conv_b.reshape(1, W),
        "wg": wg.astype(BF16),
        "bg": jnp.stack([br[0], bi[0], br[1], bi[1]]),
        "lam": lam,
        "wrest": jnp.concatenate([w_g, w_tail], axis=1).astype(BF16),
    }


def kernel(x, c, ctx, c_ctx, w_ada, b_ada, norm_w, ffn1_wi, ffn1_wo, ffn2_wi, ffn2_wo, w_in, gla_fup, gla_fb, gla_norm_w, conv_w, conv_b, lru_wr, lru_br, lru_wi, lru_bi, lru_lam, w_out_gla, w_out_lru, w_o, final_norm_w):
    B, T, D = x.shape
    Tc = ctx.shape[1]
    depth = w_ada.shape[0]
    assert depth == 1, "single trunk layer"
    assert B % SUBLANE == 0 and T % GRID_W == 0 and Tc % GLA_CHUNK == 0
    l = 0

    R = -(-(B + 1) // SUBLANE) * SUBLANE
    cs = jnp.zeros((R, D), F32).at[:B].set(c).at[B].set(c_ctx)
    mods = _ada(cs, w_ada[l], b_ada[l]).reshape(R, N_MOD, D)

    x2 = x.reshape(B * T, D)
    c2 = ctx.reshape(B * Tc, D)
    tm_lat = _pick_tile(T, 512, GRID_W)
    wi1, wo1 = ffn1_wi[l].astype(BF16), ffn1_wo[l].astype(BF16)
    wi2, wo2 = ffn2_wi[l].astype(BF16), ffn2_wo[l].astype(BF16)
    nt_lat = T // tm_lat
    tm_cf = _pick_tile(B * Tc, 512, SUBLANE)

    h = _ffn(x2, mods, lambda i: i // nt_lat, norm_w[l, 0], wi1, wo1, final_norm_w,
             sub=0, tm=tm_lat, final_norm=False)
    hc = _ffn(c2, mods, lambda i: B, norm_w[l, 0], wi1, wo1, final_norm_w,
              sub=0, tm=tm_cf, final_norm=False)

    wts = _prep_mixer_weights(w_in[l], gla_fup[l], gla_fb[l], conv_w[l], conv_b[l],
                              lru_wr[l], lru_br[l], lru_wi[l], lru_bi[l], lru_lam[l])
    W = lru_lam.shape[-1]
    ctx_o = _mix_in(hc, mods, lambda b: B, norm_w[l, 1], wts,
                    B=B, L=Tc, tm=Tc, seg=Tc, need_out=False)
    ckef, ckeb, cv, cdf, cdb, caf, cuf, cab, cub = ctx_o
    lat_o = _mix_in(h, mods, lambda b: b, norm_w[l, 1], wts,
                    B=B, L=T, tm=tm_lat, seg=GRID_W, need_out=True)
    (qdf, kif, kef, qdb, kib, keb, v, df, db, af, uf, ab, ub, sg, gy, sa, sb) = lat_o

    zero = jnp.zeros((B, W), F32)
    as_tb = lambda t, L: t.reshape(L * B, W)
    hf_c, hb_c = _lru_scan(as_tb(caf, Tc), as_tb(cuf, Tc), as_tb(cab, Tc), as_tb(cub, Tc),
                           zero, zero, B=B, need_out=False)
    hf, hb, _, _ = _lru_scan(as_tb(af, T), as_tb(uf, T), as_tb(ab, T), as_tb(ub, T),
                             hf_c, hb_c, B=B, need_out=True)

    og = _gla((ckef, ckeb, cv, cdf, cdb), (qdf, kif, kef, qdb, kib, keb, v, df, db, sg),
              gla_norm_w[l], B=B, Tc=Tc, T=T)

    oct_ = lambda t: t.reshape(T // SUBLANE, B, SUBLANE, W)
    h = _mix_out(h, mods, og, oct_(hf), oct_(hb), gy, sa, sb,
                 w_out_gla[l].astype(BF16), w_out_lru[l].astype(BF16), w_o[l].astype(BF16),
                 B=B, T=T, tm=tm_lat)

    out = _ffn(h, mods, lambda i: i // nt_lat, norm_w[l, 2], wi2, wo2, final_norm_w,
               sub=2, tm=tm_lat, final_norm=True)
    return out.reshape(B, T, D)
```

```python
import functools

import jax
import jax.numpy as jnp
from jax import lax
from jax.experimental import pallas as pl
from jax.experimental.pallas import tpu as pltpu

F32 = jnp.float32
BF16 = jnp.bfloat16

N_MOD = 9
GLA_HEADS = 4
GLA_DK = 128
GLA_DV = 256
GLA_KEY = GLA_HEADS * GLA_DK
GLA_VAL = GLA_HEADS * GLA_DV
GLA_RANK = 16
GLA_TAU = 16.0
GLA_CHUNK = 64
GRID_W = 64
LRU_BLOCKS = 8
LRU_C = 8.0
CONV_K = 4
EPS = 1e-6

LANE = 128
SUBLANE = 8
VMEM_LIMIT = 56 * 1024 * 1024


def _cparams(*sem):
    return pltpu.CompilerParams(dimension_semantics=sem, vmem_limit_bytes=VMEM_LIMIT)


def _const_spec(shape):
    nd = len(shape)
    return pl.BlockSpec(shape, lambda *_: (0,) * nd, pipeline_mode=pl.Buffered(1))


def _dot(a, b):
    return jnp.dot(a, b, preferred_element_type=F32)


def _dot_tn(a, b):
    return lax.dot_general(a, b, (((0,), (0,)), ((), ())), preferred_element_type=F32)


def _dot_nt(a, b):
    return lax.dot_general(a, b, (((1,), (1,)), ((), ())), preferred_element_type=F32)


LOG2E = 1.4426950408889634
F32_TINY = 1.1754943508222875e-38


def _sigmoid(x):
    return 1.0 / (1.0 + jnp.exp2(x * (-LOG2E)))


def _sqrt_nonneg(x):
    return x * lax.rsqrt(jnp.maximum(x, F32_TINY))


def _softplus(x):
    return jnp.maximum(x, 0.0) + jnp.log1p(jnp.exp(-jnp.abs(x)))


def _gelu_tanh(x):
    c = 0.7978845608028654
    return 0.5 * x * (1.0 + jnp.tanh(c * (x + 0.044715 * (x * x * x))))


def _rms(x):
    return x * lax.rsqrt(jnp.mean(x * x, axis=-1, keepdims=True) + EPS)


def _pick_tile(n, target, quantum):
    best = None
    t = quantum
    while t <= min(n, target):
        if n % t == 0:
            best = t
        t += quantum
    assert best is not None, (n, target, quantum)
    return best


def _ada_kernel(c_ref, w_ref, b_ref, o_ref):
    c = c_ref[...]
    s = (c * _sigmoid(c)).astype(BF16)
    o_ref[...] = _dot(s, w_ref[...].astype(BF16)) + b_ref[...]


def _ada(cs, w_ada, b_ada):
    R, D = cs.shape
    N = w_ada.shape[1]
    tn = _pick_tile(N, 2304, LANE)
    return pl.pallas_call(
        _ada_kernel,
        out_shape=jax.ShapeDtypeStruct((R, N), F32),
        grid=(N // tn,),
        in_specs=[pl.BlockSpec((R, D), lambda j: (0, 0)),
                  pl.BlockSpec((D, tn), lambda j: (0, j)),
                  pl.BlockSpec((1, tn), lambda j: (0, j))],
        out_specs=pl.BlockSpec((R, tn), lambda j: (0, j)),
        compiler_params=_cparams("parallel"),
        name="ada",
    )(cs, w_ada, b_ada.reshape(1, N))


def _ffn_kernel(x_ref, mod_ref, nw_ref, wi_ref, wo_ref, fnw_ref, o_ref, a_ref, *, sub, tf, final_norm):
    F = wo_ref.shape[0]
    x = x_ref[...]
    shift = mod_ref[3 * sub:3 * sub + 1, :]
    scale = mod_ref[3 * sub + 1:3 * sub + 2, :]
    gate = mod_ref[3 * sub + 2:3 * sub + 3, :]
    u = (_rms(x) * nw_ref[...] * (1.0 + scale) + shift).astype(BF16)
    for j in range(F // tf):
        g = _dot(u, wi_ref[:, j * tf:(j + 1) * tf])
        up = _dot(u, wi_ref[:, F + j * tf:F + (j + 1) * tf])
        a_ref[:, j * tf:(j + 1) * tf] = (g * _sigmoid(g) * up).astype(BF16)
    acc = _dot(a_ref[...], wo_ref[...])
    out = x + (0.5 * gate) * acc
    if final_norm:
        out = _rms(out) * fnw_ref[...]
    o_ref[...] = out


def _ffn(x2, mods, mod_index, nw, wi, wo, fnw, *, sub, tm, final_norm):
    rows, D = x2.shape
    F = wo.shape[0]
    tf = 2 * LANE
    assert F % tf == 0
    kern = functools.partial(_ffn_kernel, sub=sub, tf=tf, final_norm=final_norm)
    return pl.pallas_call(
        kern,
        out_shape=jax.ShapeDtypeStruct((rows, D), F32),
        scratch_shapes=[pltpu.VMEM((tm, F), BF16)],
        grid=(rows // tm,),
        in_specs=[pl.BlockSpec((tm, D), lambda i: (i, 0)),
                  pl.BlockSpec((None, N_MOD, D), lambda i: (mod_index(i), 0, 0)),
                  _const_spec((1, D)),
                  _const_spec((D, 2 * F)),
                  _const_spec((F, D)),
                  _const_spec((1, D))],
        out_specs=pl.BlockSpec((tm, D), lambda i: (i, 0)),
        compiler_params=_cparams("parallel"),
        name="ffn_sub%d" % sub,
    )(x2, mods, nw.reshape(1, D), wi, wo, fnw.reshape(1, D))


def _chunk_prefix_sum(x, pos):
    s = 1
    while s < GLA_CHUNK:
        x = x + jnp.where(pos >= s, pltpu.roll(x, s, 0), 0.0)
        s *= 2
    return x


def _mix_in_kernel(x_ref, mod_ref, nw_ref, wqkv_ref, wfd_ref, fup_ref, fb_ref, wxl_ref,
                   cw_ref, cb_ref, wrest_ref, *out_refs, seg, need_out):
    if need_out:
        (qdf_ref, kif_ref, kef_ref, qdb_ref, kib_ref, keb_ref, v_ref, decf_ref, decb_ref,
         xc_ref, sg_ref, gy_ref, sa_ref, sb_ref) = out_refs
    else:
        (kef_ref, keb_ref, v_ref, decf_ref, decb_ref, xc_ref) = out_refs
    tm = x_ref.shape[0]
    n_chunks = tm // GLA_CHUNK
    x = x_ref[...]
    shift = mod_ref[3:4, :]
    scale = mod_ref[4:5, :]
    u = (_rms(x) * nw_ref[...] * (1.0 + scale) + shift).astype(BF16)
    row = lax.broadcasted_iota(jnp.int32, (tm, 1), 0)
    cpos = row & (GLA_CHUNK - 1)
    D = x.shape[1]
    W = wxl_ref.shape[1]
    CT = 2 * LANE
    assert seg == tm or (tm % seg == 0 and seg & (seg - 1) == 0)
    spos = row if seg == tm else row & (seg - 1)

    def kc(h):
        return slice(h * GLA_DK, (h + 1) * GLA_DK)

    fd = _dot(u, wfd_ref[...]).astype(BF16)
    logits = _dot(fd, fup_ref[...]) + fb_ref[...]

    def xl_tile(j):
        return _dot(u, wxl_ref[:, j * CT:(j + 1) * CT])

    def qkv_tile(j):
        return _dot(u, wqkv_ref[:, j * CT:(j + 1) * CT])

    def rest_tile(j):
        return _dot(u, wrest_ref[:, j * CT:(j + 1) * CT])

    def decay(h, d):
        lg = logits[:, d * GLA_KEY + h * GLA_DK:d * GLA_KEY + (h + 1) * GLA_DK]
        la = (jnp.minimum(lg, 0.0) - jnp.log1p(jnp.exp(-jnp.abs(lg)))) * (LOG2E / GLA_TAU)
        cs = _chunk_prefix_sum(la, cpos)
        pre = cs if d == 0 else cs - la
        dec_ref = decf_ref if d == 0 else decb_ref
        parts = []
        for c in range(n_chunks):
            lo, hi = c * GLA_CHUNK, (c + 1) * GLA_CHUNK
            tot = cs[hi - 1:hi, :]
            parts.append(tot - pre[lo:hi, :])
            dec_ref[c * SUBLANE:(c + 1) * SUBLANE, kc(h)] = jnp.broadcast_to(jnp.exp2(tot), (SUBLANE, GLA_DK))
        return pre, jnp.concatenate(parts, axis=0)

    def conv(j, xl):
        cs = slice(j * CT, (j + 1) * CT)
        xc = cb_ref[:, cs]
        for t in range(CONV_K):
            off = t - CONV_K // 2
            if off == 0:
                term = xl
            else:
                ok = (spos + off >= 0) & (spos + off < seg)
                term = jnp.where(ok, pltpu.roll(xl, (-off) % tm, 0), 0.0)
            xc = xc + term * cw_ref[t:t + 1, cs]
        xc_ref[:, :, cs] = xc.reshape(tm // SUBLANE, SUBLANE, CT)

    tiles = {}
    dec = {}

    def key_products(h):
        k = tiles["k", h // hpt][:, (h % hpt) * GLA_DK:(h % hpt + 1) * GLA_DK]
        (b_f, e_f), (pre_b, c_b) = dec[h, 0], dec[h, 1]
        kef_ref[:, kc(h)] = (k * jnp.exp2(e_f)).astype(BF16)
        keb_ref[:, kc(h)] = (k * jnp.exp2(pre_b)).astype(BF16)
        if need_out:
            kif_ref[:, kc(h)] = (k * jnp.exp2(-b_f)).astype(BF16)
            kib_ref[:, kc(h)] = (k * jnp.exp2(-c_b)).astype(BF16)

    def query_products(h):
        q = tiles["q", h // hpt][:, (h % hpt) * GLA_DK:(h % hpt + 1) * GLA_DK] * (GLA_DK ** -0.5)
        qdf_ref[:, kc(h)] = (q * jnp.exp2(dec[h, 0][0])).astype(BF16)
        qdb_ref[:, kc(h)] = (q * jnp.exp2(dec[h, 1][1])).astype(BF16)

    n_q = GLA_KEY // CT
    n_v = GLA_VAL // CT
    hpt = CT // GLA_DK
    assert GLA_KEY % CT == 0 and GLA_VAL % CT == 0 and W % CT == 0 and D % CT == 0

    mxu, dependent, anytime = [], [], []

    def add_mxu(key, fn):
        mxu.append((key, fn))
        return len(mxu) - 1

    if need_out:
        acts = ((sg_ref, lambda t: t * _sigmoid(t), 3), (gy_ref, _gelu_tanh, 6),
                (sa_ref, _sigmoid, 2), (sb_ref, _sigmoid, 2))
        per = D // CT
        assert GLA_VAL == D and W == D
        for j in range(len(acts) * per):
            step = add_mxu(("rest", j), functools.partial(rest_tile, j))
            o_ref_, fn, cost = acts[j // per]

            def act(j=j, o_ref_=o_ref_, fn=fn):
                o_ref_[:, (j % per) * CT:(j % per + 1) * CT] = fn(tiles.pop(("rest", j))).astype(BF16)

            dependent.append((step, cost, act))
    for j in range(W // CT):
        step = add_mxu(("xl", j), functools.partial(xl_tile, j))
        dependent.append((step, 7, lambda j=j: conv(j, tiles.pop(("xl", j)))))
    for j in range(n_q):
        step = add_mxu(("k", j), functools.partial(qkv_tile, n_q + j))
        for h in range(j * hpt, (j + 1) * hpt):
            dependent.append((step, 5 if need_out else 3, functools.partial(key_products, h)))
    if need_out:
        for j in range(n_q):
            step = add_mxu(("q", j), functools.partial(qkv_tile, j))
            for h in range(j * hpt, (j + 1) * hpt):
                dependent.append((step, 3, functools.partial(query_products, h)))
    for j in range(n_v):
        step = add_mxu(("v", j), functools.partial(qkv_tile, 2 * n_q + j))

        def vcast(j=j):
            v_ref[:, j * CT:(j + 1) * CT] = tiles.pop(("v", j)).astype(BF16)

        dependent.append((step, 1, vcast))
    for h in range(GLA_HEADS):
        for d in range(2):
            def dk(h=h, d=d):
                dec[h, d] = decay(h, d)
            anytime.append((10, dk))

    total_cost = sum(c for _, c, _ in dependent) + sum(c for c, _ in anytime)
    per_step = total_cost / len(mxu)
    credit = 0.0
    for i, (key, fn) in enumerate(mxu):
        tiles[key] = fn()
        credit += per_step
        while dependent and dependent[0][0] < i and credit > 0:
            step, cost, task = dependent[0]
            if task.__class__ is functools.partial and task.func in (key_products, query_products) and anytime:
                break
            dependent.pop(0)
            task()
            credit -= cost
        while anytime and credit > 0:
            cost, task = anytime.pop(0)
            task()
            credit -= cost
    for cost, task in anytime:
        task()
    for step, cost, task in dependent:
        task()


def _mix_in(x2, mods, mod_index, nw, wts, *, B, L, tm, seg, need_out):
    rows, D = x2.shape
    W = wts["wxl"].shape[1]
    nt = L // tm
    grid = (B, nt)

    def tok(width):
        return pl.BlockSpec((tm, width), lambda b, j: (b * nt + j, 0))

    dec_spec = pl.BlockSpec((tm // SUBLANE, GLA_KEY), lambda b, j: (b * nt + j, 0))
    lru_spec = pl.BlockSpec((tm // SUBLANE, None, SUBLANE, W), lambda b, j: (j, b, 0, 0))
    tokbf = lambda width: jax.ShapeDtypeStruct((rows, width), BF16)
    dec_shape = jax.ShapeDtypeStruct((rows // SUBLANE, GLA_KEY), F32)
    lru_shape = jax.ShapeDtypeStruct((L // SUBLANE, B, SUBLANE, W), F32)

    n_key = 6 if need_out else 2
    out_shape = [tokbf(GLA_KEY)] * n_key + [tokbf(GLA_VAL), dec_shape, dec_shape, lru_shape]
    out_specs = [tok(GLA_KEY)] * n_key + [tok(GLA_VAL), dec_spec, dec_spec, lru_spec]
    if need_out:
        out_shape += [tokbf(GLA_VAL), tokbf(W), tokbf(D), tokbf(D)]
        out_specs += [tok(GLA_VAL), tok(W), tok(D), tok(D)]

    names = ["wqkv", "wfd", "fup", "fb", "wxl", "cw", "cb", "wrest"]
    w_list = [wts[n] for n in names]
    in_specs = [pl.BlockSpec((tm, D), lambda b, j: (b * nt + j, 0)),
                pl.BlockSpec((None, N_MOD, D), lambda b, j: (mod_index(b), 0, 0)),
                _const_spec((1, D))] + [_const_spec(w.shape) for w in w_list]
    kern = functools.partial(_mix_in_kernel, seg=seg, need_out=need_out)
    return pl.pallas_call(
        kern,
        out_shape=out_shape,
        grid=grid,
        in_specs=in_specs,
        out_specs=out_specs,
        compiler_params=_cparams("parallel", "parallel"),
        name="mix_in_out" if need_out else "mix_in_ctx",
    )(x2, mods, nw.reshape(1, D), *w_list)


def _lru_kernel(xf_ref, xb_ref, wg_ref, bg_ref, lam_ref, hf0_ref, hb0_ref, *refs, B, tt, need_out):
    if need_out:
        hf_ref, hb_ref, hfl_ref, hbl_ref, af_s, uf_s, ab_s, ub_s, sf_ref, sb_ref = refs
    else:
        hfl_ref, hbl_ref, af_s, uf_s, ab_s, ub_s, sf_ref, sb_ref = refs
    bw = xf_ref.shape[1]
    R = B * SUBLANE

    @pl.when(pl.program_id(1) == 0)
    def _():
        sf_ref[...] = hf0_ref[...]
        sb_ref[...] = hb0_ref[...]

    half_rate = (0.5 * LRU_C) * _softplus(-lam_ref[...])

    def gates(x_ref, d, a_s, u_s, rows):
        xc = x_ref[rows, :]
        z = _dot(xc.astype(BF16), wg_ref[:, 2 * d * bw:(2 * d + 2) * bw])
        t_r = jnp.tanh(z[:, :bw] + bg_ref[2 * d:2 * d + 1, :])
        t_i = jnp.tanh(z[:, bw:] + bg_ref[2 * d + 1:2 * d + 2, :])
        hr = half_rate[d:d + 1, :]
        y = hr + hr * t_r
        a = jnp.exp2(y * (-LOG2E))
        a_s[rows, :] = a
        xh = 0.5 * xc
        u_s[rows, :] = _sqrt_nonneg(jnp.tanh(y) * (1.0 + a * a)) * (xh + xh * t_i)

    hf = sf_ref[...]
    hb = sb_ref[...]
    n_oct = tt // SUBLANE
    for k in range(n_oct):
        kf, kb = k, n_oct - 1 - k
        gates(xf_ref, 0, af_s, uf_s, slice(kf * R, (kf + 1) * R))
        gates(xb_ref, 1, ab_s, ub_s, slice(kb * R, (kb + 1) * R))
        for t in range(SUBLANE):
            rf = pl.ds(kf * R + t, B, stride=SUBLANE)
            rb = pl.ds(kb * R + SUBLANE - 1 - t, B, stride=SUBLANE)
            hf = af_s[rf, :] * hf + uf_s[rf, :]
            hb = ab_s[rb, :] * hb + ub_s[rb, :]
            if need_out:
                hf_ref[rf, :] = hf
                hb_ref[rb, :] = hb
    sf_ref[...] = hf
    sb_ref[...] = hb
    hfl_ref[...] = hf
    hbl_ref[...] = hb


def _lru(xc, wg, bg, lam, hf0, hb0, *, B, need_out):
    rows, W = xc.shape
    L = rows // B
    tt = SUBLANE * _pick_tile(L // SUBLANE, 16, 1)
    wb = LANE
    assert W // LRU_BLOCKS == wb
    nblk = L // tt
    fwd = pl.BlockSpec((tt * B, wb), lambda w, i: (i, w))
    bwd = pl.BlockSpec((tt * B, wb), lambda w, i: (nblk - 1 - i, w))
    st = pl.BlockSpec((B, wb), lambda w, i: (0, w))
    out_shape = [jax.ShapeDtypeStruct((B, W), F32)] * 2
    out_specs = [st, st]
    if need_out:
        out_shape = [jax.ShapeDtypeStruct((rows, W), F32)] * 2 + out_shape
        out_specs = [fwd, bwd] + out_specs
    kern = functools.partial(_lru_kernel, B=B, tt=tt, need_out=need_out)
    return pl.pallas_call(
        kern,
        out_shape=out_shape,
        grid=(W // wb, nblk),
        in_specs=[fwd, bwd,
                  pl.BlockSpec((None, wb, 4 * wb), lambda w, i: (w, 0, 0)),
                  pl.BlockSpec((4, wb), lambda w, i: (0, w)),
                  pl.BlockSpec((2, wb), lambda w, i: (0, w)),
                  st, st],
        out_specs=out_specs,
        scratch_shapes=[pltpu.VMEM((tt * B, wb), F32)] * 4 + [pltpu.VMEM((B, wb), F32)] * 2,
        compiler_params=_cparams("parallel", "arbitrary"),
        name="lru_out" if need_out else "lru_ctx",
    )(xc, xc, wg, bg, lam, hf0, hb0)


def _gla_kernel(ckef_ref, ckeb_ref, cv_ref, cdf_ref, cdb_ref,
                qdf_ref, kif_ref, kef_ref, vf_ref, df_ref,
                qdb_ref, kib_ref, keb_ref, vb_ref, db_ref,
                of_ref, ob_ref, st_ref):
    C = GLA_CHUNK
    H = GLA_HEADS
    nc = cv_ref.shape[0] // C
    n = vf_ref.shape[0] // C
    ri = lax.broadcasted_iota(jnp.int32, (C, C), 0)
    ci = lax.broadcasted_iota(jnp.int32, (C, C), 1)
    causal = ci <= ri
    anti_strict = ci > ri

    def rows(c):
        return pl.ds(pl.multiple_of(c * C, C), C)

    def dec_rows(c):
        return pl.ds(pl.multiple_of(c * SUBLANE, SUBLANE), SUBLANE)

    def kc(h):
        return slice(h * GLA_DK, (h + 1) * GLA_DK)

    def vc(h):
        return slice(h * GLA_DV, (h + 1) * GLA_DV)

    def advance(idx, ke, v, dec8):
        st_ref[idx] = st_ref[idx] * dec8[0:1, :] + _dot_tn(v, ke)

    def attend(idx, qd, ki, v, mask):
        sc = jnp.where(mask, _dot_nt(qd, ki), 0.0).astype(BF16)
        return _dot(sc, v) + _dot_nt(qd, st_ref[idx].astype(BF16))

    i = pl.program_id(1)

    @pl.when(i == 0)
    def _():
        st_ref[...] = jnp.zeros(st_ref.shape, F32)

        def ctx_step(s, carry):
            rf, rb = rows(s), rows(nc - 1 - s)
            df8, db8 = cdf_ref[dec_rows(s), :], cdb_ref[dec_rows(nc - 1 - s), :]
            for h in range(H):
                advance(2 * h, ckef_ref[rf, kc(h)], cv_ref[rf, vc(h)], df8[:, kc(h)])
                advance(2 * h + 1, ckeb_ref[rb, kc(h)], cv_ref[rb, vc(h)], db8[:, kc(h)])
            return carry

        lax.fori_loop(0, nc, ctx_step, 0)

    @pl.when(i > 0)
    def _():
        def lat_step(s, carry):
            rf, rb = rows(s), rows(n - 1 - s)
            df8, db8 = df_ref[dec_rows(s), :], db_ref[dec_rows(n - 1 - s), :]
            for h in range(H):
                vf = vf_ref[rf, vc(h)]
                of_ref[rf, vc(h)] = attend(2 * h, qdf_ref[rf, kc(h)], kif_ref[rf, kc(h)], vf, causal).astype(BF16)
                advance(2 * h, kef_ref[rf, kc(h)], vf, df8[:, kc(h)])
                vb = vb_ref[rb, vc(h)]
                ob_ref[rb, vc(h)] = attend(2 * h + 1, qdb_ref[rb, kc(h)], kib_ref[rb, kc(h)], vb,
                                           anti_strict).astype(BF16)
                advance(2 * h + 1, keb_ref[rb, kc(h)], vb, db8[:, kc(h)])
            return carry

        lax.fori_loop(0, n, lat_step, 0, unroll=2)


def _gla(ctx_t, lat_t, *, B, Tc, T, tl):
    ckef, ckeb, cv, cdf, cdb = ctx_t
    qdf, kif, kef, qdb, kib, keb, v, df, db = lat_t
    nt = T // tl

    def cblk(rows_, width):
        return pl.BlockSpec((rows_, width), lambda b, i: (b, 0))

    def fblk(rows_, width):
        return pl.BlockSpec((rows_, width), lambda b, i: (b * nt + jnp.maximum(i - 1, 0), 0))

    def bblk(rows_, width):
        return pl.BlockSpec((rows_, width), lambda b, i: (b * nt + nt - 1 - jnp.maximum(i - 1, 0), 0))

    o_shape = jax.ShapeDtypeStruct((B * T, GLA_VAL), BF16)
    return pl.pallas_call(
        _gla_kernel,
        out_shape=[o_shape, o_shape],
        grid=(B, 1 + nt),
        in_specs=[cblk(Tc, GLA_KEY), cblk(Tc, GLA_KEY), cblk(Tc, GLA_VAL),
                  cblk(Tc // SUBLANE, GLA_KEY), cblk(Tc // SUBLANE, GLA_KEY),
                  fblk(tl, GLA_KEY), fblk(tl, GLA_KEY), fblk(tl, GLA_KEY), fblk(tl, GLA_VAL),
                  fblk(tl // SUBLANE, GLA_KEY),
                  bblk(tl, GLA_KEY), bblk(tl, GLA_KEY), bblk(tl, GLA_KEY), bblk(tl, GLA_VAL),
                  bblk(tl // SUBLANE, GLA_KEY)],
        out_specs=[fblk(tl, GLA_VAL), bblk(tl, GLA_VAL)],
        scratch_shapes=[pltpu.VMEM((2 * GLA_HEADS, GLA_DV, GLA_DK), F32)],
        compiler_params=_cparams("parallel", "arbitrary"),
        name="gla",
    )(ckef, ckeb, cv, cdf, cdb, qdf, kif, kef, v, df, qdb, kib, keb, v, db)


def _mix_out_kernel(x_ref, mod_ref, of_ref, ob_ref, sg_ref, gnw_ref, hf_ref, hb_ref, gy_ref, sa_ref, sb_ref,
                    wgla_ref, wlru_ref, wo_ref, o_ref):
    tm, W = gy_ref.shape
    og = []
    for h in range(GLA_HEADS):
        cs = slice(h * GLA_DV, (h + 1) * GLA_DV)
        o = of_ref[:, cs].astype(F32) + ob_ref[:, cs].astype(F32)
        og.append((_rms(o) * gnw_ref[...] * sg_ref[:, cs].astype(F32)).astype(BF16))
    y_gla = _dot(jnp.concatenate(og, axis=1), wgla_ref[...])
    hsum = (hf_ref[...] + hb_ref[...]).reshape(tm, W)
    hl = (hsum * gy_ref[...].astype(F32)).astype(BF16)
    y_lru = _dot(hl, wlru_ref[...])
    merged = sa_ref[...].astype(F32) * y_gla + sb_ref[...].astype(F32) * y_lru
    y = _dot(merged.astype(BF16), wo_ref[...])
    o_ref[...] = x_ref[...] + mod_ref[5:6, :] * y


def _mix_out(x2, mods, of, ob, sg, gnw, hf, hb, gy, sa, sb, wgla, wlru, wo, *, B, T, tm):
    rows, D = x2.shape
    W = wlru.shape[0]
    nt = T // tm
    tok = lambda width: pl.BlockSpec((tm, width), lambda b, j: (b * nt + j, 0))
    lru = pl.BlockSpec((tm // SUBLANE, None, SUBLANE, W), lambda b, j: (j, b, 0, 0))
    return pl.pallas_call(
        _mix_out_kernel,
        out_shape=jax.ShapeDtypeStruct((rows, D), F32),
        grid=(B, nt),
        in_specs=[tok(D), pl.BlockSpec((None, N_MOD, D), lambda b, j: (b, 0, 0)),
                  tok(GLA_VAL), tok(GLA_VAL), tok(GLA_VAL), _const_spec((1, GLA_DV)),
                  lru, lru, tok(W), tok(D), tok(D),
                  _const_spec(wgla.shape), _const_spec(wlru.shape), _const_spec(wo.shape)],
        out_specs=tok(D),
        compiler_params=_cparams("parallel", "parallel"),
        name="mix_out",
    )(x2, mods, of, ob, sg, gnw.reshape(1, GLA_DV), hf, hb, gy, sa, sb, wgla, wlru, wo)


def _prep_mixer_weights(w_in, fup, fb, conv_w, conv_b, wr, br, wi, bi, lam):
    D = w_in.shape[0]
    o = 0
    wqkv = w_in[:, o:o + 2 * GLA_KEY + GLA_VAL]
    o += 2 * GLA_KEY + GLA_VAL
    w_g = w_in[:, o:o + GLA_VAL]
    o += GLA_VAL
    wfd = w_in[:, o:o + 2 * GLA_RANK]
    o += 2 * GLA_RANK
    W = lam.shape[1]
    wxl = w_in[:, o:o + W]
    o += W
    w_tail = w_in[:, o:]
    wfd_p = jnp.zeros((D, LANE), F32).at[:, :2 * GLA_RANK].set(wfd)
    fup_p = jnp.zeros((LANE, 2 * GLA_KEY), F32)
    fup_p = fup_p.at[:GLA_RANK, :GLA_KEY].set(fup[0]).at[GLA_RANK:2 * GLA_RANK, GLA_KEY:].set(fup[1])
    wg = jnp.concatenate([wr[0], wi[0], wr[1], wi[1]], axis=-1)
    return {
        "wqkv": wqkv.astype(BF16),
        "wfd": wfd_p.astype(BF16),
        "fup": fup_p.astype(BF16),
        "fb": jnp.concatenate([fb[0], fb[1]]).reshape(1, 2 * GLA_KEY),
        "wxl": wxl.astype(BF16),
        "cw": conv_w,
        "cb": conv_b.reshape(1, W),
        "wg": (0.5 * wg).astype(BF16),
        "bg": 0.5 * jnp.stack([br[0], bi[0], br[1], bi[1]]),
        "lam": lam,
        "wrest": jnp.concatenate([w_g, w_tail], axis=1).astype(BF16),
    }


def kernel(x, c, ctx, c_ctx, w_ada, b_ada, norm_w, ffn1_wi, ffn1_wo, ffn2_wi, ffn2_wo, w_in, gla_fup, gla_fb, gla_norm_w, conv_w, conv_b, lru_wr, lru_br, lru_wi, lru_bi, lru_lam, w_out_gla, w_out_lru, w_o, final_norm_w):
    B, T, D = x.shape
    Tc = ctx.shape[1]
    depth = w_ada.shape[0]
    assert depth == 1, "single trunk layer"
    assert B % SUBLANE == 0 and T % GRID_W == 0 and Tc % GLA_CHUNK == 0
    l = 0

    R = -(-(B + 1) // SUBLANE) * SUBLANE
    cs = jnp.zeros((R, D), F32).at[:B].set(c).at[B].set(c_ctx)
    mods = _ada(cs, w_ada[l], b_ada[l]).reshape(R, N_MOD, D)

    x2 = x.reshape(B * T, D)
    c2 = ctx.reshape(B * Tc, D)
    tm_lat = _pick_tile(T, 512, GRID_W)
    wi1, wo1 = ffn1_wi[l].astype(BF16), ffn1_wo[l].astype(BF16)
    wi2, wo2 = ffn2_wi[l].astype(BF16), ffn2_wo[l].astype(BF16)
    nt_lat = T // tm_lat
    tm_cf = _pick_tile(B * Tc, 512, SUBLANE)

    h = _ffn(x2, mods, lambda i: i // nt_lat, norm_w[l, 0], wi1, wo1, final_norm_w,
             sub=0, tm=tm_lat, final_norm=False)
    hc = _ffn(c2, mods, lambda i: B, norm_w[l, 0], wi1, wo1, final_norm_w,
              sub=0, tm=tm_cf, final_norm=False)

    wts = _prep_mixer_weights(w_in[l], gla_fup[l], gla_fb[l], conv_w[l], conv_b[l],
                              lru_wr[l], lru_br[l], lru_wi[l], lru_bi[l], lru_lam[l])
    W = lru_lam.shape[-1]
    ctx_o = _mix_in(hc, mods, lambda b: B, norm_w[l, 1], wts,
                    B=B, L=Tc, tm=Tc, seg=Tc, need_out=False)
    ckef, ckeb, cv, cdf, cdb, cxc = ctx_o
    lat_o = _mix_in(h, mods, lambda b: b, norm_w[l, 1], wts,
                    B=B, L=T, tm=tm_lat, seg=GRID_W, need_out=True)
    (qdf, kif, kef, qdb, kib, keb, v, df, db, xc, sg, gy, sa, sb) = lat_o

    zero = jnp.zeros((B, W), F32)
    hf_c, hb_c = _lru(cxc.reshape(Tc * B, W), wts["wg"], wts["bg"], wts["lam"], zero, zero,
                      B=B, need_out=False)
    hf, hb, _, _ = _lru(xc.reshape(T * B, W), wts["wg"], wts["bg"], wts["lam"], hf_c, hb_c,
                        B=B, need_out=True)

    of, ob = _gla((ckef, ckeb, cv, cdf, cdb), (qdf, kif, kef, qdb, kib, keb, v, df, db),
                  B=B, Tc=Tc, T=T, tl=tm_lat)

    oct_ = lambda t: t.reshape(T // SUBLANE, B, SUBLANE, W)
    h = _mix_out(h, mods, of, ob, sg, gla_norm_w[l], oct_(hf), oct_(hb), gy, sa, sb,
                 w_out_gla[l].astype(BF16), w_out_lru[l].astype(BF16), w_o[l].astype(BF16),
                 B=B, T=T, tm=tm_lat)

    out = _ffn(h, mods, lambda i: i // nt_lat, norm_w[l, 2], wi2, wo2, final_norm_w,
               sub=2, tm=tm_lat, final_norm=True)
    return out.reshape(B, T, D)
```

```python
import functools

import jax
import jax.numpy as jnp
from jax import lax
from jax.experimental import pallas as pl
from jax.experimental.pallas import tpu as pltpu

F32 = jnp.float32
BF16 = jnp.bfloat16

N_MOD = 9
GLA_HEADS = 4
GLA_DK = 128
GLA_DV = 256
GLA_KEY = GLA_HEADS * GLA_DK
GLA_VAL = GLA_HEADS * GLA_DV
GLA_RANK = 16
GLA_TAU = 16.0
GLA_CHUNK = 64
GRID_W = 64
LRU_BLOCKS = 8
LRU_C = 8.0
CONV_K = 4
EPS = 1e-6

LANE = 128
SUBLANE = 8
VMEM_LIMIT = 56 * 1024 * 1024


def _cparams(*sem):
    return pltpu.CompilerParams(dimension_semantics=sem, vmem_limit_bytes=VMEM_LIMIT)


def _const_spec(shape):
    nd = len(shape)
    return pl.BlockSpec(shape, lambda *_: (0,) * nd, pipeline_mode=pl.Buffered(1))


def _dot(a, b):
    return jnp.dot(a, b, preferred_element_type=F32)


def _dot_tn(a, b):
    return lax.dot_general(a, b, (((0,), (0,)), ((), ())), preferred_element_type=F32)


def _dot_nt(a, b):
    return lax.dot_general(a, b, (((1,), (1,)), ((), ())), preferred_element_type=F32)


LOG2E = 1.4426950408889634
F32_TINY = 1.1754943508222875e-38


def _sigmoid(x):
    return 1.0 / (1.0 + jnp.exp2(x * (-LOG2E)))


def _sqrt_nonneg(x):
    return x * lax.rsqrt(jnp.maximum(x, F32_TINY))


def _softplus(x):
    return jnp.maximum(x, 0.0) + jnp.log1p(jnp.exp(-jnp.abs(x)))


def _gelu_tanh(x):
    c = 0.7978845608028654
    hx = 0.5 * x
    return hx + hx * jnp.tanh(x * (c + (c * 0.044715) * (x * x)))


def _rms(x):
    return x * lax.rsqrt(jnp.mean(x * x, axis=-1, keepdims=True) + EPS)


def _pick_tile(n, target, quantum):
    best = None
    t = quantum
    while t <= min(n, target):
        if n % t == 0:
            best = t
        t += quantum
    assert best is not None, (n, target, quantum)
    return best


def _ada_kernel(c_ref, w_ref, b_ref, o_ref):
    c = c_ref[...]
    s = (c * _sigmoid(c)).astype(BF16)
    o_ref[...] = _dot(s, w_ref[...].astype(BF16)) + b_ref[...]


def _ada(cs, w_ada, b_ada):
    R, D = cs.shape
    N = w_ada.shape[1]
    tn = _pick_tile(N, 2304, LANE)
    return pl.pallas_call(
        _ada_kernel,
        out_shape=jax.ShapeDtypeStruct((R, N), F32),
        grid=(N // tn,),
        in_specs=[pl.BlockSpec((R, D), lambda j: (0, 0)),
                  pl.BlockSpec((D, tn), lambda j: (0, j)),
                  pl.BlockSpec((1, tn), lambda j: (0, j))],
        out_specs=pl.BlockSpec((R, tn), lambda j: (0, j)),
        compiler_params=_cparams("parallel"),
        name="ada",
    )(cs, w_ada, b_ada.reshape(1, N))


def _ffn_kernel(x_ref, mod_ref, nw_ref, wi_ref, wo_ref, fnw_ref, o_ref, a_ref, *, sub, tf, final_norm):
    F = wo_ref.shape[0]
    x = x_ref[...]
    shift = mod_ref[3 * sub:3 * sub + 1, :]
    scale = mod_ref[3 * sub + 1:3 * sub + 2, :]
    gate = mod_ref[3 * sub + 2:3 * sub + 3, :]
    u = (_rms(x) * nw_ref[...] * (1.0 + scale) + shift).astype(BF16)
    for j in range(F // tf):
        g = _dot(u, wi_ref[:, j * tf:(j + 1) * tf])
        up = _dot(u, wi_ref[:, F + j * tf:F + (j + 1) * tf])
        a_ref[:, j * tf:(j + 1) * tf] = (g * _sigmoid(g) * up).astype(BF16)
    acc = _dot(a_ref[...], wo_ref[...])
    out = x + (0.5 * gate) * acc
    if final_norm:
        out = _rms(out) * fnw_ref[...]
    o_ref[...] = out


def _ffn(x2, mods, mod_index, nw, wi, wo, fnw, *, sub, tm, final_norm):
    rows, D = x2.shape
    F = wo.shape[0]
    tf = 2 * LANE
    assert F % tf == 0
    kern = functools.partial(_ffn_kernel, sub=sub, tf=tf, final_norm=final_norm)
    return pl.pallas_call(
        kern,
        out_shape=jax.ShapeDtypeStruct((rows, D), F32),
        scratch_shapes=[pltpu.VMEM((tm, F), BF16)],
        grid=(rows // tm,),
        in_specs=[pl.BlockSpec((tm, D), lambda i: (i, 0)),
                  pl.BlockSpec((None, N_MOD, D), lambda i: (mod_index(i), 0, 0)),
                  _const_spec((1, D)),
                  _const_spec((D, 2 * F)),
                  _const_spec((F, D)),
                  _const_spec((1, D))],
        out_specs=pl.BlockSpec((tm, D), lambda i: (i, 0)),
        compiler_params=_cparams("parallel"),
        name="ffn_sub%d" % sub,
    )(x2, mods, nw.reshape(1, D), wi, wo, fnw.reshape(1, D))


def _chunk_prefix_sum(x, pos):
    s = 1
    while s < GLA_CHUNK:
        x = x + jnp.where(pos >= s, pltpu.roll(x, s, 0), 0.0)
        s *= 2
    return x


def _mix_in_kernel(x_ref, mod_ref, nw_ref, wqkv_ref, wfd_ref, fup_ref, fb_ref, wxl_ref,
                   cw_ref, cb_ref, wrest_ref, *out_refs, seg, need_out):
    if need_out:
        (qdf_ref, kif_ref, kef_ref, qdb_ref, kib_ref, keb_ref, v_ref, decf_ref, decb_ref,
         xc_ref, sg_ref, gy_ref, sa_ref, sb_ref) = out_refs
    else:
        (kef_ref, keb_ref, v_ref, decf_ref, decb_ref, xc_ref) = out_refs
    tm = x_ref.shape[0]
    n_chunks = tm // GLA_CHUNK
    x = x_ref[...]
    shift = mod_ref[3:4, :]
    scale = mod_ref[4:5, :]
    u = (_rms(x) * nw_ref[...] * (1.0 + scale) + shift).astype(BF16)
    row = lax.broadcasted_iota(jnp.int32, (tm, 1), 0)
    cpos = row & (GLA_CHUNK - 1)
    D = x.shape[1]
    W = wxl_ref.shape[1]
    CT = 2 * LANE
    assert tm % seg == 0 and seg % SUBLANE == 0 and cw_ref.shape[1] == seg

    def kc(h):
        return slice(h * GLA_DK, (h + 1) * GLA_DK)

    fd = _dot(u, wfd_ref[...]).astype(BF16)
    logits = _dot(fd, fup_ref[...]) + fb_ref[...]

    def xl_tile(j):
        return _dot(u, wxl_ref[:, j * CT:(j + 1) * CT])

    def qkv_tile(j):
        return _dot(u, wqkv_ref[:, j * CT:(j + 1) * CT])

    def rest_tile(j):
        return _dot(u, wrest_ref[:, j * CT:(j + 1) * CT])

    def decay(h, d):
        lg = logits[:, d * GLA_KEY + h * GLA_DK:d * GLA_KEY + (h + 1) * GLA_DK]
        z = lg * LOG2E
        la = (jnp.minimum(z, 0.0) - jnp.log2(1.0 + jnp.exp2(-jnp.abs(z)))) * (1.0 / GLA_TAU)
        cs = _chunk_prefix_sum(la, cpos)
        pre = cs if d == 0 else cs - la
        dec_ref = decf_ref if d == 0 else decb_ref
        parts = []
        for c in range(n_chunks):
            lo, hi = c * GLA_CHUNK, (c + 1) * GLA_CHUNK
            tot = cs[hi - 1:hi, :]
            parts.append(tot - pre[lo:hi, :])
            dec_ref[c * SUBLANE:(c + 1) * SUBLANE, kc(h)] = jnp.broadcast_to(jnp.exp2(tot), (SUBLANE, GLA_DK))
        return pre, jnp.concatenate(parts, axis=0)

    def conv(j, xl):
        cs = slice(j * CT, (j + 1) * CT)
        xc = cb_ref[:, cs][None]
        for t in range(CONV_K):
            off = t - CONV_K // 2
            term = xl if off == 0 else pltpu.roll(xl, (-off) % tm, 0)
            xc = xc + term.reshape(tm // seg, seg, CT) * cw_ref[t, :, cs][None]
        xc_ref[:, :, cs] = xc.reshape(tm // SUBLANE, SUBLANE, CT)

    tiles = {}
    dec = {}

    def key_products(h):
        k = tiles["k", h // hpt][:, (h % hpt) * GLA_DK:(h % hpt + 1) * GLA_DK]
        (b_f, e_f), (pre_b, c_b) = dec[h, 0], dec[h, 1]
        kef_ref[:, kc(h)] = (k * jnp.exp2(e_f)).astype(BF16)
        keb_ref[:, kc(h)] = (k * jnp.exp2(pre_b)).astype(BF16)
        if need_out:
            kif_ref[:, kc(h)] = (k * jnp.exp2(-b_f)).astype(BF16)
            kib_ref[:, kc(h)] = (k * jnp.exp2(-c_b)).astype(BF16)

    def query_products(h):
        q = tiles["q", h // hpt][:, (h % hpt) * GLA_DK:(h % hpt + 1) * GLA_DK] * (GLA_DK ** -0.5)
        qdf_ref[:, kc(h)] = (q * jnp.exp2(dec[h, 0][0])).astype(BF16)
        qdb_ref[:, kc(h)] = (q * jnp.exp2(dec[h, 1][1])).astype(BF16)

    n_q = GLA_KEY // CT
    n_v = GLA_VAL // CT
    hpt = CT // GLA_DK
    assert GLA_KEY % CT == 0 and GLA_VAL % CT == 0 and W % CT == 0 and D % CT == 0

    mxu, dependent, anytime = [], [], []

    def add_mxu(key, fn):
        mxu.append((key, fn))
        return len(mxu) - 1

    if need_out:
        acts = ((sg_ref, lambda t: t * _sigmoid(t), 3), (gy_ref, _gelu_tanh, 6),
                (sa_ref, _sigmoid, 2), (sb_ref, _sigmoid, 2))
        per = D // CT
        assert GLA_VAL == D and W == D
        for j in range(len(acts) * per):
            step = add_mxu(("rest", j), functools.partial(rest_tile, j))
            o_ref_, fn, cost = acts[j // per]

            def act(j=j, o_ref_=o_ref_, fn=fn):
                o_ref_[:, (j % per) * CT:(j % per + 1) * CT] = fn(tiles.pop(("rest", j))).astype(BF16)

            dependent.append((step, cost, act))
    for j in range(W // CT):
        step = add_mxu(("xl", j), functools.partial(xl_tile, j))
        dependent.append((step, 7, lambda j=j: conv(j, tiles.pop(("xl", j)))))
    for j in range(n_q):
        step = add_mxu(("k", j), functools.partial(qkv_tile, n_q + j))
        for h in range(j * hpt, (j + 1) * hpt):
            dependent.append((step, 5 if need_out else 3, functools.partial(key_products, h)))
    if need_out:
        for j in range(n_q):
            step = add_mxu(("q", j), functools.partial(qkv_tile, j))
            for h in range(j * hpt, (j + 1) * hpt):
                dependent.append((step, 3, functools.partial(query_products, h)))
    for j in range(n_v):
        step = add_mxu(("v", j), functools.partial(qkv_tile, 2 * n_q + j))

        def vcast(j=j):
            v_ref[:, j * CT:(j + 1) * CT] = tiles.pop(("v", j)).astype(BF16)

        dependent.append((step, 1, vcast))
    for h in range(GLA_HEADS):
        for d in range(2):
            def dk(h=h, d=d):
                dec[h, d] = decay(h, d)
            anytime.append((10, dk))

    total_cost = sum(c for _, c, _ in dependent) + sum(c for c, _ in anytime)
    per_step = total_cost / len(mxu)
    credit = 0.0
    for i, (key, fn) in enumerate(mxu):
        tiles[key] = fn()
        credit += per_step
        while dependent and dependent[0][0] < i and credit > 0:
            step, cost, task = dependent[0]
            if task.__class__ is functools.partial and task.func in (key_products, query_products) and anytime:
                break
            dependent.pop(0)
            task()
            credit -= cost
        while anytime and credit > 0:
            cost, task = anytime.pop(0)
            task()
            credit -= cost
    for cost, task in anytime:
        task()
    for step, cost, task in dependent:
        task()


def _segment_tap_weights(conv_w, seg):
    pos = jnp.arange(seg)[None, :, None]
    off = (jnp.arange(CONV_K) - CONV_K // 2)[:, None, None]
    inside = (pos + off >= 0) & (pos + off < seg)
    return jnp.where(inside, conv_w[:, None, :], 0.0)


def _mix_in(x2, mods, mod_index, nw, wts, *, B, L, tm, seg, need_out):
    rows, D = x2.shape
    W = wts["wxl"].shape[1]
    nt = L // tm
    grid = (B, nt)

    def tok(width):
        return pl.BlockSpec((tm, width), lambda b, j: (b * nt + j, 0))

    dec_spec = pl.BlockSpec((tm // SUBLANE, GLA_KEY), lambda b, j: (b * nt + j, 0))
    lru_spec = pl.BlockSpec((tm // SUBLANE, None, SUBLANE, W), lambda b, j: (j, b, 0, 0))
    tokbf = lambda width: jax.ShapeDtypeStruct((rows, width), BF16)
    dec_shape = jax.ShapeDtypeStruct((rows // SUBLANE, GLA_KEY), F32)
    lru_shape = jax.ShapeDtypeStruct((L // SUBLANE, B, SUBLANE, W), F32)

    n_key = 6 if need_out else 2
    out_shape = [tokbf(GLA_KEY)] * n_key + [tokbf(GLA_VAL), dec_shape, dec_shape, lru_shape]
    out_specs = [tok(GLA_KEY)] * n_key + [tok(GLA_VAL), dec_spec, dec_spec, lru_spec]
    if need_out:
        out_shape += [tokbf(GLA_VAL), tokbf(W), tokbf(D), tokbf(D)]
        out_specs += [tok(GLA_VAL), tok(W), tok(D), tok(D)]

    names = ["wqkv", "wfd", "fup", "fb", "wxl", "cw", "cb", "wrest"]
    w_list = [_segment_tap_weights(wts["cw"], seg) if n == "cw" else wts[n] for n in names]
    in_specs = [pl.BlockSpec((tm, D), lambda b, j: (b * nt + j, 0)),
                pl.BlockSpec((None, N_MOD, D), lambda b, j: (mod_index(b), 0, 0)),
                _const_spec((1, D))] + [_const_spec(w.shape) for w in w_list]
    kern = functools.partial(_mix_in_kernel, seg=seg, need_out=need_out)
    return pl.pallas_call(
        kern,
        out_shape=out_shape,
        grid=grid,
        in_specs=in_specs,
        out_specs=out_specs,
        compiler_params=_cparams("parallel", "parallel"),
        name="mix_in_out" if need_out else "mix_in_ctx",
    )(x2, mods, nw.reshape(1, D), *w_list)


def _lru_kernel(xf_ref, xb_ref, wg_ref, bg_ref, lam_ref, hf0_ref, hb0_ref, *refs, B, tt, need_out):
    if need_out:
        hf_ref, hb_ref, hfl_ref, hbl_ref, af_s, uf_s, ab_s, ub_s, sf_ref, sb_ref = refs
    else:
        hfl_ref, hbl_ref, af_s, uf_s, ab_s, ub_s, sf_ref, sb_ref = refs
    bw = xf_ref.shape[1]
    R = B * SUBLANE

    @pl.when(pl.program_id(1) == 0)
    def _():
        sf_ref[...] = hf0_ref[...]
        sb_ref[...] = hb0_ref[...]

    half_rate = (0.5 * LRU_C) * _softplus(-lam_ref[...])

    def gates(x_ref, d, a_s, u_s, rows):
        xc = x_ref[rows, :]
        z = _dot(xc.astype(BF16), wg_ref[:, 2 * d * bw:(2 * d + 2) * bw])
        t_r = jnp.tanh(z[:, :bw] + bg_ref[2 * d:2 * d + 1, :])
        t_i = jnp.tanh(z[:, bw:] + bg_ref[2 * d + 1:2 * d + 2, :])
        hr = half_rate[d:d + 1, :]
        y = hr + hr * t_r
        a = jnp.exp2(y * (-LOG2E))
        a_s[rows, :] = a
        xh = 0.5 * xc
        u_s[rows, :] = _sqrt_nonneg(jnp.tanh(y) * (1.0 + a * a)) * (xh + xh * t_i)

    hf = sf_ref[...]
    hb = sb_ref[...]
    n_oct = tt // SUBLANE
    for k in range(n_oct):
        kf, kb = k, n_oct - 1 - k
        gates(xf_ref, 0, af_s, uf_s, slice(kf * R, (kf + 1) * R))
        gates(xb_ref, 1, ab_s, ub_s, slice(kb * R, (kb + 1) * R))
        for t in range(SUBLANE):
            rf = pl.ds(kf * R + t, B, stride=SUBLANE)
            rb = pl.ds(kb * R + SUBLANE - 1 - t, B, stride=SUBLANE)
            hf = af_s[rf, :] * hf + uf_s[rf, :]
            hb = ab_s[rb, :] * hb + ub_s[rb, :]
            if need_out:
                hf_ref[rf, :] = hf
                hb_ref[rb, :] = hb
    sf_ref[...] = hf
    sb_ref[...] = hb
    hfl_ref[...] = hf
    hbl_ref[...] = hb


def _lru(xc, wg, bg, lam, hf0, hb0, *, B, need_out):
    rows, W = xc.shape
    L = rows // B
    tt = SUBLANE * _pick_tile(L // SUBLANE, 16, 1)
    wb = LANE
    assert W // LRU_BLOCKS == wb
    nblk = L // tt
    fwd = pl.BlockSpec((tt * B, wb), lambda w, i: (i, w))
    bwd = pl.BlockSpec((tt * B, wb), lambda w, i: (nblk - 1 - i, w))
    st = pl.BlockSpec((B, wb), lambda w, i: (0, w))
    out_shape = [jax.ShapeDtypeStruct((B, W), F32)] * 2
    out_specs = [st, st]
    if need_out:
        out_shape = [jax.ShapeDtypeStruct((rows, W), F32)] * 2 + out_shape
        out_specs = [fwd, bwd] + out_specs
    kern = functools.partial(_lru_kernel, B=B, tt=tt, need_out=need_out)
    return pl.pallas_call(
        kern,
        out_shape=out_shape,
        grid=(W // wb, nblk),
        in_specs=[fwd, bwd,
                  pl.BlockSpec((None, wb, 4 * wb), lambda w, i: (w, 0, 0)),
                  pl.BlockSpec((4, wb), lambda w, i: (0, w)),
                  pl.BlockSpec((2, wb), lambda w, i: (0, w)),
                  st, st],
        out_specs=out_specs,
        scratch_shapes=[pltpu.VMEM((tt * B, wb), F32)] * 4 + [pltpu.VMEM((B, wb), F32)] * 2,
        compiler_params=_cparams("parallel", "arbitrary"),
        name="lru_out" if need_out else "lru_ctx",
    )(xc, xc, wg, bg, lam, hf0, hb0)


def _gla_kernel(ckef_ref, ckeb_ref, cv_ref, cdf_ref, cdb_ref,
                qdf_ref, kef_ref, vf_ref, df_ref,
                qdb_ref, keb_ref, vb_ref, db_ref,
                of_ref, ob_ref, st_ref):
    C = GLA_CHUNK
    H = GLA_HEADS
    nc = cv_ref.shape[0] // C
    n = vf_ref.shape[0] // C

    def rows(c):
        return pl.ds(pl.multiple_of(c * C, C), C)

    def dec_rows(c):
        return pl.ds(pl.multiple_of(c * SUBLANE, SUBLANE), SUBLANE)

    def kc(h):
        return slice(h * GLA_DK, (h + 1) * GLA_DK)

    def vc(h):
        return slice(h * GLA_DV, (h + 1) * GLA_DV)

    def advance(idx, ke, v, dec8):
        st_ref[idx] = st_ref[idx] * dec8[0:1, :] + _dot_tn(v, ke)

    def attend(idx, qd):
        return _dot_nt(qd, st_ref[idx].astype(BF16)).astype(BF16)

    i = pl.program_id(1)

    @pl.when(i == 0)
    def _():
        st_ref[...] = jnp.zeros(st_ref.shape, F32)

        def ctx_step(s, carry):
            rf, rb = rows(s), rows(nc - 1 - s)
            df8, db8 = cdf_ref[dec_rows(s), :], cdb_ref[dec_rows(nc - 1 - s), :]
            for h in range(H):
                advance(2 * h, ckef_ref[rf, kc(h)], cv_ref[rf, vc(h)], df8[:, kc(h)])
                advance(2 * h + 1, ckeb_ref[rb, kc(h)], cv_ref[rb, vc(h)], db8[:, kc(h)])
            return carry

        lax.fori_loop(0, nc, ctx_step, 0)

    @pl.when(i > 0)
    def _():
        def lat_step(s, carry):
            rf, rb = rows(s), rows(n - 1 - s)
            df8, db8 = df_ref[dec_rows(s), :], db_ref[dec_rows(n - 1 - s), :]
            for h in range(H):
                of_ref[rf, vc(h)] = attend(2 * h, qdf_ref[rf, kc(h)])
                advance(2 * h, kef_ref[rf, kc(h)], vf_ref[rf, vc(h)], df8[:, kc(h)])
                ob_ref[rb, vc(h)] = attend(2 * h + 1, qdb_ref[rb, kc(h)])
                advance(2 * h + 1, keb_ref[rb, kc(h)], vb_ref[rb, vc(h)], db8[:, kc(h)])
            return carry

        lax.fori_loop(0, n, lat_step, 0, unroll=2)


def _gla(ctx_t, lat_t, *, B, Tc, T, tl):
    ckef, ckeb, cv, cdf, cdb = ctx_t
    qdf, kef, qdb, keb, v, df, db = lat_t
    nt = T // tl

    def cblk(rows_, width):
        return pl.BlockSpec((rows_, width), lambda b, i: (b, 0))

    def fblk(rows_, width):
        return pl.BlockSpec((rows_, width), lambda b, i: (b * nt + jnp.maximum(i - 1, 0), 0))

    def bblk(rows_, width):
        return pl.BlockSpec((rows_, width), lambda b, i: (b * nt + nt - 1 - jnp.maximum(i - 1, 0), 0))

    o_shape = jax.ShapeDtypeStruct((B * T, GLA_VAL), BF16)
    return pl.pallas_call(
        _gla_kernel,
        out_shape=[o_shape, o_shape],
        grid=(B, 1 + nt),
        in_specs=[cblk(Tc, GLA_KEY), cblk(Tc, GLA_KEY), cblk(Tc, GLA_VAL),
                  cblk(Tc // SUBLANE, GLA_KEY), cblk(Tc // SUBLANE, GLA_KEY),
                  fblk(tl, GLA_KEY), fblk(tl, GLA_KEY), fblk(tl, GLA_VAL), fblk(tl // SUBLANE, GLA_KEY),
                  bblk(tl, GLA_KEY), bblk(tl, GLA_KEY), bblk(tl, GLA_VAL), bblk(tl // SUBLANE, GLA_KEY)],
        out_specs=[fblk(tl, GLA_VAL), bblk(tl, GLA_VAL)],
        scratch_shapes=[pltpu.VMEM((2 * GLA_HEADS, GLA_DV, GLA_DK), F32)],
        compiler_params=_cparams("parallel", "arbitrary"),
        name="gla",
    )(ckef, ckeb, cv, cdf, cdb, qdf, kef, v, df, qdb, keb, v, db)


def _mix_out_kernel(x_ref, mod_ref, of_ref, ob_ref, qdf_ref, kif_ref, qdb_ref, kib_ref, v_ref,
                    sg_ref, gnw_ref, hf_ref, hb_ref, gy_ref, sa_ref, sb_ref,
                    wgla_ref, wlru_ref, wo_ref, o_ref, oi_ref):
    tm, W = gy_ref.shape
    R = _pick_tile(tm, 2 * LANE, GLA_CHUNK)
    ri = lax.broadcasted_iota(jnp.int32, (R, R), 0)
    ci = lax.broadcasted_iota(jnp.int32, (R, R), 1)
    same = (ri // GLA_CHUNK) == (ci // GLA_CHUNK)
    fwd_ok = same & (ci <= ri)
    bwd_ok = same & (ci > ri)
    for h in range(GLA_HEADS):
        ks = slice(h * GLA_DK, (h + 1) * GLA_DK)
        cs = slice(h * GLA_DV, (h + 1) * GLA_DV)
        for c in range(tm // R):
            r = slice(c * R, (c + 1) * R)
            sf = _dot_nt(qdf_ref[r, ks], kif_ref[r, ks])
            sb = _dot_nt(qdb_ref[r, ks], kib_ref[r, ks])
            sc = jnp.where(fwd_ok, sf, jnp.where(bwd_ok, sb, 0.0))
            oi_ref[r, cs] = _dot(sc.astype(BF16), v_ref[r, cs])
    og = []
    for h in range(GLA_HEADS):
        cs = slice(h * GLA_DV, (h + 1) * GLA_DV)
        o = oi_ref[:, cs] + of_ref[:, cs].astype(F32) + ob_ref[:, cs].astype(F32)
        og.append((_rms(o) * gnw_ref[...] * sg_ref[:, cs].astype(F32)).astype(BF16))
    y_gla = _dot(jnp.concatenate(og, axis=1), wgla_ref[...])
    hsum = (hf_ref[...] + hb_ref[...]).reshape(tm, W)
    hl = (hsum * gy_ref[...].astype(F32)).astype(BF16)
    y_lru = _dot(hl, wlru_ref[...])
    merged = sa_ref[...].astype(F32) * y_gla + sb_ref[...].astype(F32) * y_lru
    y = _dot(merged.astype(BF16), wo_ref[...])
    o_ref[...] = x_ref[...] + mod_ref[5:6, :] * y


def _mix_out(x2, mods, of, ob, qdf, kif, qdb, kib, v, sg, gnw, hf, hb, gy, sa, sb, wgla, wlru, wo, *, B, T, tm):
    rows, D = x2.shape
    W = wlru.shape[0]
    nt = T // tm
    tok = lambda width: pl.BlockSpec((tm, width), lambda b, j: (b * nt + j, 0))
    lru = pl.BlockSpec((tm // SUBLANE, None, SUBLANE, W), lambda b, j: (j, b, 0, 0))
    return pl.pallas_call(
        _mix_out_kernel,
        out_shape=jax.ShapeDtypeStruct((rows, D), F32),
        grid=(B, nt),
        in_specs=[tok(D), pl.BlockSpec((None, N_MOD, D), lambda b, j: (b, 0, 0)),
                  tok(GLA_VAL), tok(GLA_VAL),
                  tok(GLA_KEY), tok(GLA_KEY), tok(GLA_KEY), tok(GLA_KEY), tok(GLA_VAL),
                  tok(GLA_VAL), _const_spec((1, GLA_DV)),
                  lru, lru, tok(W), tok(D), tok(D),
                  _const_spec(wgla.shape), _const_spec(wlru.shape), _const_spec(wo.shape)],
        out_specs=tok(D),
        scratch_shapes=[pltpu.VMEM((tm, GLA_VAL), F32)],
        compiler_params=_cparams("parallel", "parallel"),
        name="mix_out",
    )(x2, mods, of, ob, qdf, kif, qdb, kib, v, sg, gnw.reshape(1, GLA_DV), hf, hb, gy, sa, sb, wgla, wlru, wo)


def _prep_mixer_weights(w_in, fup, fb, conv_w, conv_b, wr, br, wi, bi, lam):
    D = w_in.shape[0]
    o = 0
    wqkv = w_in[:, o:o + 2 * GLA_KEY + GLA_VAL]
    o += 2 * GLA_KEY + GLA_VAL
    w_g = w_in[:, o:o + GLA_VAL]
    o += GLA_VAL
    wfd = w_in[:, o:o + 2 * GLA_RANK]
    o += 2 * GLA_RANK
    W = lam.shape[1]
    wxl = w_in[:, o:o + W]
    o += W
    w_tail = w_in[:, o:]
    wfd_p = jnp.zeros((D, LANE), F32).at[:, :2 * GLA_RANK].set(wfd)
    fup_p = jnp.zeros((LANE, 2 * GLA_KEY), F32)
    fup_p = fup_p.at[:GLA_RANK, :GLA_KEY].set(fup[0]).at[GLA_RANK:2 * GLA_RANK, GLA_KEY:].set(fup[1])
    wg = jnp.concatenate([wr[0], wi[0], wr[1], wi[1]], axis=-1)
    return {
        "wqkv": wqkv.astype(BF16),
        "wfd": wfd_p.astype(BF16),
        "fup": fup_p.astype(BF16),
        "fb": jnp.concatenate([fb[0], fb[1]]).reshape(1, 2 * GLA_KEY),
        "wxl": wxl.astype(BF16),
        "cw": conv_w,
        "cb": conv_b.reshape(1, W),
        "wg": (0.5 * wg).astype(BF16),
        "bg": 0.5 * jnp.stack([br[0], bi[0], br[1], bi[1]]),
        "lam": lam,
        "wrest": jnp.concatenate([w_g, w_tail], axis=1).astype(BF16),
    }


def kernel(x, c, ctx, c_ctx, w_ada, b_ada, norm_w, ffn1_wi, ffn1_wo, ffn2_wi, ffn2_wo, w_in, gla_fup, gla_fb, gla_norm_w, conv_w, conv_b, lru_wr, lru_br, lru_wi, lru_bi, lru_lam, w_out_gla, w_out_lru, w_o, final_norm_w):
    B, T, D = x.shape
    Tc = ctx.shape[1]
    depth = w_ada.shape[0]
    assert depth == 1, "single trunk layer"
    assert B % SUBLANE == 0 and T % GRID_W == 0 and Tc % GLA_CHUNK == 0
    l = 0

    R = -(-(B + 1) // SUBLANE) * SUBLANE
    cs = jnp.zeros((R, D), F32).at[:B].set(c).at[B].set(c_ctx)
    mods = _ada(cs, w_ada[l], b_ada[l]).reshape(R, N_MOD, D)

    x2 = x.reshape(B * T, D)
    c2 = ctx.reshape(B * Tc, D)
    tm_lat = _pick_tile(T, 512, GRID_W)
    wi1, wo1 = ffn1_wi[l].astype(BF16), ffn1_wo[l].astype(BF16)
    wi2, wo2 = ffn2_wi[l].astype(BF16), ffn2_wo[l].astype(BF16)
    nt_lat = T // tm_lat
    tm_cf = _pick_tile(B * Tc, 512, SUBLANE)

    h = _ffn(x2, mods, lambda i: i // nt_lat, norm_w[l, 0], wi1, wo1, final_norm_w,
             sub=0, tm=tm_lat, final_norm=False)
    hc = _ffn(c2, mods, lambda i: B, norm_w[l, 0], wi1, wo1, final_norm_w,
              sub=0, tm=tm_cf, final_norm=False)

    wts = _prep_mixer_weights(w_in[l], gla_fup[l], gla_fb[l], conv_w[l], conv_b[l],
                              lru_wr[l], lru_br[l], lru_wi[l], lru_bi[l], lru_lam[l])
    W = lru_lam.shape[-1]
    ctx_o = _mix_in(hc, mods, lambda b: B, norm_w[l, 1], wts,
                    B=B, L=Tc, tm=Tc, seg=Tc, need_out=False)
    ckef, ckeb, cv, cdf, cdb, cxc = ctx_o
    lat_o = _mix_in(h, mods, lambda b: b, norm_w[l, 1], wts,
                    B=B, L=T, tm=tm_lat, seg=GRID_W, need_out=True)
    (qdf, kif, kef, qdb, kib, keb, v, df, db, xc, sg, gy, sa, sb) = lat_o

    zero = jnp.zeros((B, W), F32)
    hf_c, hb_c = _lru(cxc.reshape(Tc * B, W), wts["wg"], wts["bg"], wts["lam"], zero, zero,
                      B=B, need_out=False)
    hf, hb, _, _ = _lru(xc.reshape(T * B, W), wts["wg"], wts["bg"], wts["lam"], hf_c, hb_c,
                        B=B, need_out=True)

    of, ob = _gla((ckef, ckeb, cv, cdf, cdb), (qdf, kef, qdb, keb, v, df, db),
                  B=B, Tc=Tc, T=T, tl=tm_lat)

    oct_ = lambda t: t.reshape(T // SUBLANE, B, SUBLANE, W)
    h = _mix_out(h, mods, of, ob, qdf, kif, qdb, kib, v, sg, gla_norm_w[l], oct_(hf), oct_(hb), gy, sa, sb,
                 w_out_gla[l].astype(BF16), w_out_lru[l].astype(BF16), w_o[l].astype(BF16),
                 B=B, T=T, tm=tm_lat)

    out = _ffn(h, mods, lambda i: i // nt_lat, norm_w[l, 2], wi2, wo2, final_norm_w,
               sub=2, tm=tm_lat, final_norm=True)
    return out.reshape(B, T, D)
```

```python
import functools

import jax
import jax.numpy as jnp
from jax import lax
from jax.experimental import pallas as pl
from jax.experimental.pallas import tpu as pltpu

F32 = jnp.float32
BF16 = jnp.bfloat16

N_MOD = 9
GLA_HEADS = 4
GLA_DK = 128
GLA_DV = 256
GLA_KEY = GLA_HEADS * GLA_DK
GLA_VAL = GLA_HEADS * GLA_DV
GLA_RANK = 16
GLA_TAU = 16.0
GLA_CHUNK = 64
GRID_W = 64
LRU_BLOCKS = 8
LRU_C = 8.0
CONV_K = 4
EPS = 1e-6

LANE = 128
SUBLANE = 8
VMEM_LIMIT = 56 * 1024 * 1024


def _cparams(*sem):
    return pltpu.CompilerParams(dimension_semantics=sem, vmem_limit_bytes=VMEM_LIMIT)


def _const_spec(shape):
    nd = len(shape)
    return pl.BlockSpec(shape, lambda *_: (0,) * nd, pipeline_mode=pl.Buffered(1))


def _dot(a, b):
    return jnp.dot(a, b, preferred_element_type=F32)


def _dot_tn(a, b):
    return lax.dot_general(a, b, (((0,), (0,)), ((), ())), preferred_element_type=F32)


def _dot_nt(a, b):
    return lax.dot_general(a, b, (((1,), (1,)), ((), ())), preferred_element_type=F32)


LOG2E = 1.4426950408889634
F32_TINY = 1.1754943508222875e-38


def _sigmoid(x):
    return 1.0 / (1.0 + jnp.exp2(x * (-LOG2E)))


def _sqrt_nonneg(x):
    return x * lax.rsqrt(jnp.maximum(x, F32_TINY))


def _softplus(x):
    return jnp.maximum(x, 0.0) + jnp.log1p(jnp.exp(-jnp.abs(x)))


def _gelu_tanh(x):
    c = 0.7978845608028654
    hx = 0.5 * x
    return hx + hx * jnp.tanh(x * (c + (c * 0.044715) * (x * x)))


def _rms(x):
    return x * lax.rsqrt(jnp.mean(x * x, axis=-1, keepdims=True) + EPS)


def _pick_tile(n, target, quantum):
    best = None
    t = quantum
    while t <= min(n, target):
        if n % t == 0:
            best = t
        t += quantum
    assert best is not None, (n, target, quantum)
    return best


def _ada_kernel(c_ref, w_ref, b_ref, o_ref):
    c = c_ref[...]
    s = (c * _sigmoid(c)).astype(BF16)
    o_ref[...] = _dot(s, w_ref[...].astype(BF16)) + b_ref[...]


def _ada(cs, w_ada, b_ada):
    R, D = cs.shape
    N = w_ada.shape[1]
    tn = _pick_tile(N, 2304, LANE)
    return pl.pallas_call(
        _ada_kernel,
        out_shape=jax.ShapeDtypeStruct((R, N), F32),
        grid=(N // tn,),
        in_specs=[pl.BlockSpec((R, D), lambda j: (0, 0)),
                  pl.BlockSpec((D, tn), lambda j: (0, j)),
                  pl.BlockSpec((1, tn), lambda j: (0, j))],
        out_specs=pl.BlockSpec((R, tn), lambda j: (0, j)),
        compiler_params=_cparams("parallel"),
        name="ada",
    )(cs, w_ada, b_ada.reshape(1, N))


def _ffn_kernel(x_ref, mod_ref, nw_ref, wi_ref, wo_ref, fnw_ref, o_ref, a_ref, *, sub, tf, final_norm):
    F = wo_ref.shape[0]
    x = x_ref[...]
    shift = mod_ref[3 * sub:3 * sub + 1, :]
    scale = mod_ref[3 * sub + 1:3 * sub + 2, :]
    gate = mod_ref[3 * sub + 2:3 * sub + 3, :]
    u = (_rms(x) * nw_ref[...] * (1.0 + scale) + shift).astype(BF16)
    for j in range(F // tf):
        g = _dot(u, wi_ref[:, j * tf:(j + 1) * tf])
        up = _dot(u, wi_ref[:, F + j * tf:F + (j + 1) * tf])
        a_ref[:, j * tf:(j + 1) * tf] = (g * _sigmoid(g) * up).astype(BF16)
    acc = _dot(a_ref[...], wo_ref[...])
    out = x + (0.5 * gate) * acc
    if final_norm:
        out = _rms(out) * fnw_ref[...]
    o_ref[...] = out


def _ffn(x2, mods, mod_index, nw, wi, wo, fnw, *, sub, tm, final_norm):
    rows, D = x2.shape
    F = wo.shape[0]
    tf = 2 * LANE
    assert F % tf == 0
    kern = functools.partial(_ffn_kernel, sub=sub, tf=tf, final_norm=final_norm)
    return pl.pallas_call(
        kern,
        out_shape=jax.ShapeDtypeStruct((rows, D), F32),
        scratch_shapes=[pltpu.VMEM((tm, F), BF16)],
        grid=(rows // tm,),
        in_specs=[pl.BlockSpec((tm, D), lambda i: (i, 0)),
                  pl.BlockSpec((None, N_MOD, D), lambda i: (mod_index(i), 0, 0)),
                  _const_spec((1, D)),
                  _const_spec((D, 2 * F)),
                  _const_spec((F, D)),
                  _const_spec((1, D))],
        out_specs=pl.BlockSpec((tm, D), lambda i: (i, 0)),
        compiler_params=_cparams("parallel"),
        name="ffn_sub%d" % sub,
    )(x2, mods, nw.reshape(1, D), wi, wo, fnw.reshape(1, D))


def _chunk_prefix_sum(x, pos):
    s = 1
    while s < GLA_CHUNK:
        x = x + jnp.where(pos >= s, pltpu.roll(x, s, 0), 0.0)
        s *= 2
    return x


def _mix_in_kernel(x_ref, mod_ref, nw_ref, wqkv_ref, wfd_ref, fup_ref, fb_ref, wxl_ref,
                   cw_ref, cb_ref, wrest_ref, *out_refs, seg, need_out):
    if need_out:
        (qdf_ref, kif_ref, kef_ref, qdb_ref, kib_ref, keb_ref, v_ref, decf_ref, decb_ref,
         xc_ref, sg_ref, gy_ref, sa_ref, sb_ref) = out_refs
    else:
        (kef_ref, keb_ref, v_ref, decf_ref, decb_ref, xc_ref) = out_refs
    tm = x_ref.shape[0]
    n_chunks = tm // GLA_CHUNK
    x = x_ref[...]
    shift = mod_ref[3:4, :]
    scale = mod_ref[4:5, :]
    u = (_rms(x) * nw_ref[...] * (1.0 + scale) + shift).astype(BF16)
    row = lax.broadcasted_iota(jnp.int32, (tm, 1), 0)
    cpos = row & (GLA_CHUNK - 1)
    D = x.shape[1]
    W = wxl_ref.shape[1]
    CT = 2 * LANE
    assert tm % seg == 0 and seg % SUBLANE == 0 and cw_ref.shape[1] == seg

    def kc(h):
        return slice(h * GLA_DK, (h + 1) * GLA_DK)

    fd = _dot(u, wfd_ref[...]).astype(BF16)
    logits = _dot(fd, fup_ref[...]) + fb_ref[...]

    def xl_tile(j):
        return _dot(u, wxl_ref[:, j * CT:(j + 1) * CT])

    def qkv_tile(j):
        return _dot(u, wqkv_ref[:, j * CT:(j + 1) * CT])

    def rest_tile(j):
        return _dot(u, wrest_ref[:, j * CT:(j + 1) * CT])

    sub_pos = lax.broadcasted_iota(jnp.int32, (1, SUBLANE, 1), 1)
    vregs_per_chunk = GLA_CHUNK // SUBLANE

    def chunk_log_decay(r, h, d):
        lg = logits[r, d * GLA_KEY + h * GLA_DK:d * GLA_KEY + (h + 1) * GLA_DK]
        z = lg * LOG2E
        la = (jnp.minimum(z, 0.0) - jnp.log2(1.0 + jnp.exp2(-jnp.abs(z)))) * (1.0 / GLA_TAU)
        la = la.reshape(vregs_per_chunk, SUBLANE, GLA_DK)
        cs = la
        s = 1
        while s < SUBLANE:
            cs = cs + jnp.where(sub_pos >= s, pltpu.roll(cs, s, 1), 0.0)
            s *= 2
        last = cs[:, SUBLANE - 1:SUBLANE, :]
        carry = [jnp.zeros((1, 1, GLA_DK), F32)]
        for g in range(1, vregs_per_chunk):
            carry.append(carry[-1] + last[g - 1:g])
        tot = carry[-1] + last[vregs_per_chunk - 1:]
        return la, cs + jnp.concatenate(carry, axis=0), tot

    def conv(j, xl):
        cs = slice(j * CT, (j + 1) * CT)
        xc = cb_ref[:, cs][None]
        for t in range(CONV_K):
            off = t - CONV_K // 2
            term = xl if off == 0 else pltpu.roll(xl, (-off) % tm, 0)
            xc = xc + term.reshape(tm // seg, seg, CT) * cw_ref[t, :, cs][None]
        xc_ref[:, :, cs] = xc.reshape(tm // SUBLANE, SUBLANE, CT)

    tiles = {}

    def decayed_products(h, c):
        r = slice(c * GLA_CHUNK, (c + 1) * GLA_CHUNK)
        cols = slice((h % hpt) * GLA_DK, (h % hpt + 1) * GLA_DK)
        shape3 = (vregs_per_chunk, SUBLANE, GLA_DK)
        _, b_f, tot_f = chunk_log_decay(r, h, 0)
        la_b, cs_b, tot_b = chunk_log_decay(r, h, 1)
        pre_b = cs_b - la_b
        c_b = tot_b - pre_b
        e_f = tot_f - b_f
        decf_ref[c * SUBLANE:(c + 1) * SUBLANE, kc(h)] = jnp.broadcast_to(jnp.exp2(tot_f[0]), (SUBLANE, GLA_DK))
        decb_ref[c * SUBLANE:(c + 1) * SUBLANE, kc(h)] = jnp.broadcast_to(jnp.exp2(tot_b[0]), (SUBLANE, GLA_DK))

        def put(ref, val):
            ref[r, kc(h)] = val.reshape(GLA_CHUNK, GLA_DK).astype(BF16)

        k = tiles["k", h // hpt][r, cols].reshape(shape3)
        put(kef_ref, k * jnp.exp2(e_f))
        put(keb_ref, k * jnp.exp2(pre_b))
        if need_out:
            put(kif_ref, k * jnp.exp2(-b_f))
            put(kib_ref, k * jnp.exp2(-c_b))
            q = tiles["q", h // hpt][r, cols].reshape(shape3) * (GLA_DK ** -0.5)
            put(qdf_ref, q * jnp.exp2(b_f))
            put(qdb_ref, q * jnp.exp2(c_b))

    n_q = GLA_KEY // CT
    n_v = GLA_VAL // CT
    hpt = CT // GLA_DK
    assert GLA_KEY % CT == 0 and GLA_VAL % CT == 0 and W % CT == 0 and D % CT == 0

    mxu, pending = [], []

    def add_mxu(key, fn):
        mxu.append((key, fn))
        return len(mxu) - 1

    for j in range(W // CT):
        step = add_mxu(("xl", j), functools.partial(xl_tile, j))
        pending.append((step, 7, lambda j=j: conv(j, tiles.pop(("xl", j)))))
    last_qk = 0
    for j in range(n_q):
        last_qk = add_mxu(("k", j), functools.partial(qkv_tile, n_q + j))
    if need_out:
        for j in range(n_q):
            last_qk = add_mxu(("q", j), functools.partial(qkv_tile, j))
    for h in range(GLA_HEADS):
        for c in range(n_chunks):
            pending.append((last_qk, 3 if need_out else 2, functools.partial(decayed_products, h, c)))
    for j in range(n_v):
        step = add_mxu(("v", j), functools.partial(qkv_tile, 2 * n_q + j))

        def vcast(j=j):
            v_ref[:, j * CT:(j + 1) * CT] = tiles.pop(("v", j)).astype(BF16)

        pending.append((step, 1, vcast))
    if need_out:
        acts = ((sg_ref, lambda t: t * _sigmoid(t), 3), (gy_ref, _gelu_tanh, 6),
                (sa_ref, _sigmoid, 2), (sb_ref, _sigmoid, 2))
        per = D // CT
        assert GLA_VAL == D and W == D
        for j in range(len(acts) * per):
            step = add_mxu(("rest", j), functools.partial(rest_tile, j))
            o_ref_, fn, cost = acts[j // per]

            def act(j=j, o_ref_=o_ref_, fn=fn):
                o_ref_[:, (j % per) * CT:(j % per + 1) * CT] = fn(tiles.pop(("rest", j))).astype(BF16)

            pending.append((step, cost, act))

    bulk = [p for p in pending if p[2].__class__ is functools.partial]
    quick = [p for p in pending if p[2].__class__ is not functools.partial]
    quick.sort(key=lambda p: p[0])
    per_step = sum(c for _, c, _ in pending) / len(mxu)
    credit = 0.0
    for i, (key, fn) in enumerate(mxu):
        tiles[key] = fn()
        credit += per_step
        for queue in (quick, bulk):
            while queue and queue[0][0] < i and credit > 0:
                step, cost, task = queue.pop(0)
                task()
                credit -= cost
    for step, cost, task in quick + bulk:
        task()


def _segment_tap_weights(conv_w, seg):
    pos = jnp.arange(seg)[None, :, None]
    off = (jnp.arange(CONV_K) - CONV_K // 2)[:, None, None]
    inside = (pos + off >= 0) & (pos + off < seg)
    return jnp.where(inside, conv_w[:, None, :], 0.0)


def _mix_in(x2, mods, mod_index, nw, wts, *, B, L, tm, seg, need_out):
    rows, D = x2.shape
    W = wts["wxl"].shape[1]
    nt = L // tm
    grid = (B, nt)

    def tok(width):
        return pl.BlockSpec((tm, width), lambda b, j: (b * nt + j, 0))

    dec_spec = pl.BlockSpec((tm // SUBLANE, GLA_KEY), lambda b, j: (b * nt + j, 0))
    lru_spec = pl.BlockSpec((tm // SUBLANE, None, SUBLANE, W), lambda b, j: (j, b, 0, 0))
    tokbf = lambda width: jax.ShapeDtypeStruct((rows, width), BF16)
    dec_shape = jax.ShapeDtypeStruct((rows // SUBLANE, GLA_KEY), F32)
    lru_shape = jax.ShapeDtypeStruct((L // SUBLANE, B, SUBLANE, W), F32)

    n_key = 6 if need_out else 2
    out_shape = [tokbf(GLA_KEY)] * n_key + [tokbf(GLA_VAL), dec_shape, dec_shape, lru_shape]
    out_specs = [tok(GLA_KEY)] * n_key + [tok(GLA_VAL), dec_spec, dec_spec, lru_spec]
    if need_out:
        out_shape += [tokbf(GLA_VAL), tokbf(W), tokbf(D), tokbf(D)]
        out_specs += [tok(GLA_VAL), tok(W), tok(D), tok(D)]

    names = ["wqkv", "wfd", "fup", "fb", "wxl", "cw", "cb", "wrest"]
    w_list = [_segment_tap_weights(wts["cw"], seg) if n == "cw" else wts[n] for n in names]
    in_specs = [pl.BlockSpec((tm, D), lambda b, j: (b * nt + j, 0)),
                pl.BlockSpec((None, N_MOD, D), lambda b, j: (mod_index(b), 0, 0)),
                _const_spec((1, D))] + [_const_spec(w.shape) for w in w_list]
    kern = functools.partial(_mix_in_kernel, seg=seg, need_out=need_out)
    return pl.pallas_call(
        kern,
        out_shape=out_shape,
        grid=grid,
        in_specs=in_specs,
        out_specs=out_specs,
        compiler_params=_cparams("parallel", "parallel"),
        name="mix_in_out" if need_out else "mix_in_ctx",
    )(x2, mods, nw.reshape(1, D), *w_list)


def _lru_kernel(xf_ref, xb_ref, wg_ref, lam_ref, hf0_ref, hb0_ref, *refs, B, tt, need_out):
    if need_out:
        hf_ref, hb_ref, hfl_ref, hbl_ref, af_s, uf_s, ab_s, ub_s, sf_ref, sb_ref = refs
    else:
        hfl_ref, hbl_ref, af_s, uf_s, ab_s, ub_s, sf_ref, sb_ref = refs
    bw = xf_ref.shape[1]
    R = B * SUBLANE

    @pl.when(pl.program_id(1) == 0)
    def _():
        sf_ref[...] = hf0_ref[...]
        sb_ref[...] = hb0_ref[...]

    half_rate = (0.5 * LRU_C) * _softplus(-lam_ref[...])
    lane = lax.broadcasted_iota(jnp.int32, (R, bw), 1)
    ones = jnp.where(lane < 2, 1.0, 0.0).astype(BF16)

    def gates(x_ref, d, a_s, u_s, rows):
        xc = x_ref[rows, :]
        lhs = jnp.concatenate([xc.astype(BF16), ones], axis=1)
        z = _dot(lhs, wg_ref[:, 2 * d * bw:(2 * d + 2) * bw])
        t_r = jnp.tanh(z[:, :bw])
        t_i = jnp.tanh(z[:, bw:])
        hr = half_rate[d:d + 1, :]
        y = hr + hr * t_r
        a = jnp.exp2(y * (-LOG2E))
        a_s[rows, :] = a
        xh = 0.5 * xc
        u_s[rows, :] = _sqrt_nonneg(1.0 - a * a) * (xh + xh * t_i)

    hf = sf_ref[...]
    hb = sb_ref[...]
    n_oct = tt // SUBLANE
    for k in range(n_oct):
        kf, kb = k, n_oct - 1 - k
        gates(xf_ref, 0, af_s, uf_s, slice(kf * R, (kf + 1) * R))
        gates(xb_ref, 1, ab_s, ub_s, slice(kb * R, (kb + 1) * R))
        for t in range(SUBLANE):
            rf = pl.ds(kf * R + t, B, stride=SUBLANE)
            rb = pl.ds(kb * R + SUBLANE - 1 - t, B, stride=SUBLANE)
            hf = af_s[rf, :] * hf + uf_s[rf, :]
            hb = ab_s[rb, :] * hb + ub_s[rb, :]
            if need_out:
                hf_ref[rf, :] = hf
                hb_ref[rb, :] = hb
    sf_ref[...] = hf
    sb_ref[...] = hb
    hfl_ref[...] = hf
    hbl_ref[...] = hb


def _lru(xc, wg, lam, hf0, hb0, *, B, need_out):
    rows, W = xc.shape
    L = rows // B
    tt = SUBLANE * _pick_tile(L // SUBLANE, 16, 1)
    wb = LANE
    assert W // LRU_BLOCKS == wb
    nblk = L // tt
    fwd = pl.BlockSpec((tt * B, wb), lambda w, i: (i, w))
    bwd = pl.BlockSpec((tt * B, wb), lambda w, i: (nblk - 1 - i, w))
    st = pl.BlockSpec((B, wb), lambda w, i: (0, w))
    out_shape = [jax.ShapeDtypeStruct((B, W), F32)] * 2
    out_specs = [st, st]
    if need_out:
        out_shape = [jax.ShapeDtypeStruct((rows, W), F32)] * 2 + out_shape
        out_specs = [fwd, bwd] + out_specs
    kern = functools.partial(_lru_kernel, B=B, tt=tt, need_out=need_out)
    return pl.pallas_call(
        kern,
        out_shape=out_shape,
        grid=(W // wb, nblk),
        in_specs=[fwd, bwd,
                  pl.BlockSpec((None, 2 * wb, 4 * wb), lambda w, i: (w, 0, 0)),
                  pl.BlockSpec((2, wb), lambda w, i: (0, w)),
                  st, st],
        out_specs=out_specs,
        scratch_shapes=[pltpu.VMEM((tt * B, wb), F32)] * 4 + [pltpu.VMEM((B, wb), F32)] * 2,
        compiler_params=_cparams("parallel", "arbitrary"),
        name="lru_out" if need_out else "lru_ctx",
    )(xc, xc, wg, lam, hf0, hb0)


def _gla_kernel(ckef_ref, ckeb_ref, cv_ref, cdf_ref, cdb_ref,
                qdf_ref, kef_ref, vf_ref, df_ref,
                qdb_ref, keb_ref, vb_ref, db_ref,
                of_ref, ob_ref, st_ref):
    C = GLA_CHUNK
    H = GLA_HEADS
    nc = cv_ref.shape[0] // C
    n = vf_ref.shape[0] // C

    def rows(c):
        return pl.ds(pl.multiple_of(c * C, C), C)

    def dec_rows(c):
        return pl.ds(pl.multiple_of(c * SUBLANE, SUBLANE), SUBLANE)

    def kc(h):
        return slice(h * GLA_DK, (h + 1) * GLA_DK)

    def vc(h):
        return slice(h * GLA_DV, (h + 1) * GLA_DV)

    def advance(idx, ke, v, dec8):
        st_ref[idx] = st_ref[idx] * dec8[0:1, :] + _dot_tn(v, ke)

    def attend(idx, qd):
        return _dot_nt(qd, st_ref[idx].astype(BF16)).astype(BF16)

    i = pl.program_id(1)

    @pl.when(i == 0)
    def _():
        st_ref[...] = jnp.zeros(st_ref.shape, F32)

        def ctx_step(s, carry):
            rf, rb = rows(s), rows(nc - 1 - s)
            df8, db8 = cdf_ref[dec_rows(s), :], cdb_ref[dec_rows(nc - 1 - s), :]
            for h in range(H):
                advance(2 * h, ckef_ref[rf, kc(h)], cv_ref[rf, vc(h)], df8[:, kc(h)])
                advance(2 * h + 1, ckeb_ref[rb, kc(h)], cv_ref[rb, vc(h)], db8[:, kc(h)])
            return carry

        lax.fori_loop(0, nc, ctx_step, 0, unroll=2)

    @pl.when(i > 0)
    def _():
        def lat_step(s, carry):
            rf, rb = rows(s), rows(n - 1 - s)
            df8, db8 = df_ref[dec_rows(s), :], db_ref[dec_rows(n - 1 - s), :]
            for h in range(H):
                of_ref[rf, vc(h)] = attend(2 * h, qdf_ref[rf, kc(h)])
                advance(2 * h, kef_ref[rf, kc(h)], vf_ref[rf, vc(h)], df8[:, kc(h)])
                ob_ref[rb, vc(h)] = attend(2 * h + 1, qdb_ref[rb, kc(h)])
                advance(2 * h + 1, keb_ref[rb, kc(h)], vb_ref[rb, vc(h)], db8[:, kc(h)])
            return carry

        lax.fori_loop(0, n, lat_step, 0, unroll=4)


def _gla(ctx_t, lat_t, *, B, Tc, T, tl):
    ckef, ckeb, cv, cdf, cdb = ctx_t
    qdf, kef, qdb, keb, v, df, db = lat_t
    nt = T // tl

    def cblk(rows_, width):
        return pl.BlockSpec((rows_, width), lambda b, i: (b, 0))

    def fblk(rows_, width):
        return pl.BlockSpec((rows_, width), lambda b, i: (b * nt + jnp.maximum(i - 1, 0), 0))

    def bblk(rows_, width):
        return pl.BlockSpec((rows_, width), lambda b, i: (b * nt + nt - 1 - jnp.maximum(i - 1, 0), 0))

    o_shape = jax.ShapeDtypeStruct((B * T, GLA_VAL), BF16)
    return pl.pallas_call(
        _gla_kernel,
        out_shape=[o_shape, o_shape],
        grid=(B, 1 + nt),
        in_specs=[cblk(Tc, GLA_KEY), cblk(Tc, GLA_KEY), cblk(Tc, GLA_VAL),
                  cblk(Tc // SUBLANE, GLA_KEY), cblk(Tc // SUBLANE, GLA_KEY),
                  fblk(tl, GLA_KEY), fblk(tl, GLA_KEY), fblk(tl, GLA_VAL), fblk(tl // SUBLANE, GLA_KEY),
                  bblk(tl, GLA_KEY), bblk(tl, GLA_KEY), bblk(tl, GLA_VAL), bblk(tl // SUBLANE, GLA_KEY)],
        out_specs=[fblk(tl, GLA_VAL), bblk(tl, GLA_VAL)],
        scratch_shapes=[pltpu.VMEM((2 * GLA_HEADS, GLA_DV, GLA_DK), F32)],
        compiler_params=_cparams("parallel", "arbitrary"),
        name="gla",
    )(ckef, ckeb, cv, cdf, cdb, qdf, kef, v, df, qdb, keb, v, db)


def _mix_out_kernel(x_ref, mod_ref, of_ref, ob_ref, qdf_ref, kif_ref, qdb_ref, kib_ref, v_ref,
                    sg_ref, gnw_ref, hf_ref, hb_ref, gy_ref, sa_ref, sb_ref,
                    wgla_ref, wlru_ref, wo_ref, o_ref, oi_ref):
    tm, W = gy_ref.shape
    hsum = (hf_ref[...] + hb_ref[...]).reshape(tm, W)
    hl = (hsum * gy_ref[...].astype(F32)).astype(BF16)
    y_lru = _dot(hl, wlru_ref[...])
    R = _pick_tile(tm, 2 * LANE, GLA_CHUNK)
    ri = lax.broadcasted_iota(jnp.int32, (R, R), 0)
    ci = lax.broadcasted_iota(jnp.int32, (R, R), 1)
    same = (ri // GLA_CHUNK) == (ci // GLA_CHUNK)
    fwd_ok = same & (ci <= ri)
    bwd_ok = same & (ci > ri)
    og = []
    for h in range(GLA_HEADS):
        ks = slice(h * GLA_DK, (h + 1) * GLA_DK)
        cs = slice(h * GLA_DV, (h + 1) * GLA_DV)
        for c in range(tm // R):
            r = slice(c * R, (c + 1) * R)
            sf = _dot_nt(qdf_ref[r, ks], kif_ref[r, ks])
            sb = _dot_nt(qdb_ref[r, ks], kib_ref[r, ks])
            sc = jnp.where(fwd_ok, sf, jnp.where(bwd_ok, sb, 0.0))
            oi_ref[r, cs] = _dot(sc.astype(BF16), v_ref[r, cs])
        o = oi_ref[:, cs] + of_ref[:, cs].astype(F32) + ob_ref[:, cs].astype(F32)
        og.append((_rms(o) * gnw_ref[...] * sg_ref[:, cs].astype(F32)).astype(BF16))
    y_gla = _dot(jnp.concatenate(og, axis=1), wgla_ref[...])
    merged = sa_ref[...].astype(F32) * y_gla + sb_ref[...].astype(F32) * y_lru
    y = _dot(merged.astype(BF16), wo_ref[...])
    o_ref[...] = x_ref[...] + mod_ref[5:6, :] * y


def _mix_out(x2, mods, of, ob, qdf, kif, qdb, kib, v, sg, gnw, hf, hb, gy, sa, sb, wgla, wlru, wo, *, B, T, tm):
    rows, D = x2.shape
    W = wlru.shape[0]
    nt = T // tm
    tok = lambda width: pl.BlockSpec((tm, width), lambda b, j: (b * nt + j, 0))
    lru = pl.BlockSpec((tm // SUBLANE, None, SUBLANE, W), lambda b, j: (j, b, 0, 0))
    return pl.pallas_call(
        _mix_out_kernel,
        out_shape=jax.ShapeDtypeStruct((rows, D), F32),
        grid=(B, nt),
        in_specs=[tok(D), pl.BlockSpec((None, N_MOD, D), lambda b, j: (b, 0, 0)),
                  tok(GLA_VAL), tok(GLA_VAL),
                  tok(GLA_KEY), tok(GLA_KEY), tok(GLA_KEY), tok(GLA_KEY), tok(GLA_VAL),
                  tok(GLA_VAL), _const_spec((1, GLA_DV)),
                  lru, lru, tok(W), tok(D), tok(D),
                  _const_spec(wgla.shape), _const_spec(wlru.shape), _const_spec(wo.shape)],
        out_specs=tok(D),
        scratch_shapes=[pltpu.VMEM((tm, GLA_VAL), F32)],
        compiler_params=_cparams("parallel", "parallel"),
        name="mix_out",
    )(x2, mods, of, ob, qdf, kif, qdb, kib, v, sg, gnw.reshape(1, GLA_DV), hf, hb, gy, sa, sb, wgla, wlru, wo)


def _prep_mixer_weights(w_in, fup, fb, conv_w, conv_b, wr, br, wi, bi, lam):
    D = w_in.shape[0]
    o = 0
    wqkv = w_in[:, o:o + 2 * GLA_KEY + GLA_VAL]
    o += 2 * GLA_KEY + GLA_VAL
    w_g = w_in[:, o:o + GLA_VAL]
    o += GLA_VAL
    wfd = w_in[:, o:o + 2 * GLA_RANK]
    o += 2 * GLA_RANK
    W = lam.shape[1]
    wxl = w_in[:, o:o + W]
    o += W
    w_tail = w_in[:, o:]
    wfd_p = jnp.zeros((D, LANE), F32).at[:, :2 * GLA_RANK].set(wfd)
    fup_p = jnp.zeros((LANE, 2 * GLA_KEY), F32)
    fup_p = fup_p.at[:GLA_RANK, :GLA_KEY].set(fup[0]).at[GLA_RANK:2 * GLA_RANK, GLA_KEY:].set(fup[1])
    wg = 0.5 * jnp.concatenate([wr[0], wi[0], wr[1], wi[1]], axis=-1)
    bw = W // LRU_BLOCKS
    bg = 0.5 * jnp.stack([br[0], bi[0], br[1], bi[1]])
    bg = bg.reshape(4, LRU_BLOCKS, bw).transpose(1, 0, 2).reshape(LRU_BLOCKS, 1, 4 * bw)
    bg_hi = bg.astype(BF16)
    bg_lo = (bg - bg_hi.astype(F32)).astype(BF16)
    wg_aug = jnp.concatenate([wg.astype(BF16), bg_hi, bg_lo,
                              jnp.zeros((LRU_BLOCKS, bw - 2, 4 * bw), BF16)], axis=1)
    return {
        "wqkv": wqkv.astype(BF16),
        "wfd": wfd_p.astype(BF16),
        "fup": fup_p.astype(BF16),
        "fb": jnp.concatenate([fb[0], fb[1]]).reshape(1, 2 * GLA_KEY),
        "wxl": wxl.astype(BF16),
        "cw": conv_w,
        "cb": conv_b.reshape(1, W),
        "wg": wg_aug,
        "lam": lam,
        "wrest": jnp.concatenate([w_g, w_tail], axis=1).astype(BF16),
    }


def kernel(x, c, ctx, c_ctx, w_ada, b_ada, norm_w, ffn1_wi, ffn1_wo, ffn2_wi, ffn2_wo, w_in, gla_fup, gla_fb, gla_norm_w, conv_w, conv_b, lru_wr, lru_br, lru_wi, lru_bi, lru_lam, w_out_gla, w_out_lru, w_o, final_norm_w):
    B, T, D = x.shape
    Tc = ctx.shape[1]
    depth = w_ada.shape[0]
    assert depth == 1, "single trunk layer"
    assert B % SUBLANE == 0 and T % GRID_W == 0 and Tc % GLA_CHUNK == 0
    l = 0

    R = -(-(B + 1) // SUBLANE) * SUBLANE
    cs = jnp.zeros((R, D), F32).at[:B].set(c).at[B].set(c_ctx)
    mods = _ada(cs, w_ada[l], b_ada[l]).reshape(R, N_MOD, D)

    x2 = x.reshape(B * T, D)
    c2 = ctx.reshape(B * Tc, D)
    tm_lat = _pick_tile(T, 512, GRID_W)
    wi1, wo1 = ffn1_wi[l].astype(BF16), ffn1_wo[l].astype(BF16)
    wi2, wo2 = ffn2_wi[l].astype(BF16), ffn2_wo[l].astype(BF16)
    tm_ffn = _pick_tile(T, 1024, GRID_W)
    nt_ffn = T // tm_ffn
    tm_cf = _pick_tile(B * Tc, 1024, SUBLANE)

    h = _ffn(x2, mods, lambda i: i // nt_ffn, norm_w[l, 0], wi1, wo1, final_norm_w,
             sub=0, tm=tm_ffn, final_norm=False)
    hc = _ffn(c2, mods, lambda i: B, norm_w[l, 0], wi1, wo1, final_norm_w,
              sub=0, tm=tm_cf, final_norm=False)

    wts = _prep_mixer_weights(w_in[l], gla_fup[l], gla_fb[l], conv_w[l], conv_b[l],
                              lru_wr[l], lru_br[l], lru_wi[l], lru_bi[l], lru_lam[l])
    W = lru_lam.shape[-1]
    ctx_o = _mix_in(hc, mods, lambda b: B, norm_w[l, 1], wts,
                    B=B, L=Tc, tm=Tc, seg=Tc, need_out=False)
    ckef, ckeb, cv, cdf, cdb, cxc = ctx_o
    lat_o = _mix_in(h, mods, lambda b: b, norm_w[l, 1], wts,
                    B=B, L=T, tm=tm_lat, seg=GRID_W, need_out=True)
    (qdf, kif, kef, qdb, kib, keb, v, df, db, xc, sg, gy, sa, sb) = lat_o

    zero = jnp.zeros((B, W), F32)
    hf_c, hb_c = _lru(cxc.reshape(Tc * B, W), wts["wg"], wts["lam"], zero, zero,
                      B=B, need_out=False)
    hf, hb, _, _ = _lru(xc.reshape(T * B, W), wts["wg"], wts["lam"], hf_c, hb_c,
                        B=B, need_out=True)

    of, ob = _gla((ckef, ckeb, cv, cdf, cdb), (qdf, kef, qdb, keb, v, df, db),
                  B=B, Tc=Tc, T=T, tl=tm_lat)

    oct_ = lambda t: t.reshape(T // SUBLANE, B, SUBLANE, W)
    h = _mix_out(h, mods, of, ob, qdf, kif, qdb, kib, v, sg, gla_norm_w[l], oct_(hf), oct_(hb), gy, sa, sb,
                 w_out_gla[l].astype(BF16), w_out_lru[l].astype(BF16), w_o[l].astype(BF16),
                 B=B, T=T, tm=tm_lat)

    out = _ffn(h, mods, lambda i: i // nt_ffn, norm_w[l, 2], wi2, wo2, final_norm_w,
               sub=2, tm=tm_ffn, final_norm=True)
    return out.reshape(B, T, D)
```

```python
import functools

import jax
import jax.numpy as jnp
from jax import lax
from jax.experimental import pallas as pl
from jax.experimental.pallas import tpu as pltpu

F32 = jnp.float32
BF16 = jnp.bfloat16

N_MOD = 9
GLA_HEADS = 4
GLA_DK = 128
GLA_DV = 256
GLA_KEY = GLA_HEADS * GLA_DK
GLA_VAL = GLA_HEADS * GLA_DV
GLA_RANK = 16
GLA_TAU = 16.0
GLA_CHUNK = 64
GRID_W = 64
LRU_BLOCKS = 8
LRU_C = 8.0
CONV_K = 4
EPS = 1e-6

LANE = 128
SUBLANE = 8
VMEM_LIMIT = 56 * 1024 * 1024


def _cparams(*sem):
    return pltpu.CompilerParams(dimension_semantics=sem, vmem_limit_bytes=VMEM_LIMIT)


def _const_spec(shape):
    nd = len(shape)
    return pl.BlockSpec(shape, lambda *_: (0,) * nd, pipeline_mode=pl.Buffered(1))


def _dot(a, b):
    return jnp.dot(a, b, preferred_element_type=F32)


def _dot_tn(a, b):
    return lax.dot_general(a, b, (((0,), (0,)), ((), ())), preferred_element_type=F32)


def _dot_nt(a, b):
    return lax.dot_general(a, b, (((1,), (1,)), ((), ())), preferred_element_type=F32)


LOG2E = 1.4426950408889634
F32_TINY = 1.1754943508222875e-38


def _sigmoid(x):
    return 1.0 / (1.0 + jnp.exp2(x * (-LOG2E)))


def _sqrt_nonneg(x):
    return x * lax.rsqrt(jnp.maximum(x, F32_TINY))


def _softplus(x):
    return jnp.maximum(x, 0.0) + jnp.log1p(jnp.exp(-jnp.abs(x)))


def _gelu_tanh(x):
    c = 0.7978845608028654
    hx = 0.5 * x
    return hx + hx * jnp.tanh(x * (c + (c * 0.044715) * (x * x)))


def _rms(x):
    return x * lax.rsqrt(jnp.mean(x * x, axis=-1, keepdims=True) + EPS)


def _pick_tile(n, target, quantum):
    best = None
    t = quantum
    while t <= min(n, target):
        if n % t == 0:
            best = t
        t += quantum
    assert best is not None, (n, target, quantum)
    return best


def _ada_kernel(c_ref, w_ref, b_ref, o_ref):
    c = c_ref[...]
    s = (c * _sigmoid(c)).astype(BF16)
    o_ref[...] = _dot(s, w_ref[...].astype(BF16)) + b_ref[...]


def _ada(cs, w_ada, b_ada, layer):
    R, D = cs.shape
    N = w_ada.shape[2]
    tn = _pick_tile(N, 2304, LANE)
    return pl.pallas_call(
        _ada_kernel,
        out_shape=jax.ShapeDtypeStruct((R, N), F32),
        grid=(N // tn,),
        in_specs=[pl.BlockSpec((R, D), lambda j: (0, 0)),
                  pl.BlockSpec((None, D, tn), lambda j: (layer, 0, j)),
                  pl.BlockSpec((1, tn), lambda j: (0, j))],
        out_specs=pl.BlockSpec((R, tn), lambda j: (0, j)),
        compiler_params=_cparams("parallel"),
        name="ada",
    )(cs, w_ada, b_ada[layer].reshape(1, N))


def _ffn_kernel(x_ref, mod_ref, nw_ref, wi_ref, wo_ref, fnw_ref, o_ref, a_ref, *, sub, tf, final_norm):
    F = wo_ref.shape[0]
    x = x_ref[...]
    shift = mod_ref[3 * sub:3 * sub + 1, :]
    scale = mod_ref[3 * sub + 1:3 * sub + 2, :]
    gate = mod_ref[3 * sub + 2:3 * sub + 3, :]
    u = (_rms(x) * nw_ref[...] * (1.0 + scale) + shift).astype(BF16)
    for j in range(F // tf):
        g = _dot(u, wi_ref[:, j * tf:(j + 1) * tf])
        up = _dot(u, wi_ref[:, F + j * tf:F + (j + 1) * tf])
        a_ref[:, j * tf:(j + 1) * tf] = (g * _sigmoid(g) * up).astype(BF16)
    acc = _dot(a_ref[...], wo_ref[...])
    out = x + (0.5 * gate) * acc
    if final_norm:
        out = _rms(out) * fnw_ref[...]
    o_ref[...] = out


def _ffn(x2, mods, mod_index, nw, wi, wo, fnw, *, sub, tm, final_norm):
    rows, D = x2.shape
    F = wo.shape[0]
    tf = 2 * LANE
    assert F % tf == 0
    kern = functools.partial(_ffn_kernel, sub=sub, tf=tf, final_norm=final_norm)
    return pl.pallas_call(
        kern,
        out_shape=jax.ShapeDtypeStruct((rows, D), F32),
        scratch_shapes=[pltpu.VMEM((tm, F), BF16)],
        grid=(rows // tm,),
        in_specs=[pl.BlockSpec((tm, D), lambda i: (i, 0)),
                  pl.BlockSpec((None, N_MOD, D), lambda i: (mod_index(i), 0, 0)),
                  _const_spec((1, D)),
                  _const_spec((D, 2 * F)),
                  _const_spec((F, D)),
                  _const_spec((1, D))],
        out_specs=pl.BlockSpec((tm, D), lambda i: (i, 0)),
        compiler_params=_cparams("parallel"),
        name="ffn_sub%d" % sub,
    )(x2, mods, nw.reshape(1, D), wi, wo, fnw.reshape(1, D))


def _mix_in_kernel(x_ref, mod_ref, nw_ref, wqkv_ref, wfd_ref, fup_ref, fb_ref, wxl_ref,
                   cw_ref, cb_ref, wrest_ref, *out_refs, seg, need_out):
    if need_out:
        (qdf_ref, kif_ref, kef_ref, qdb_ref, kib_ref, keb_ref, v_ref, decf_ref, decb_ref,
         xc_ref, sg_ref, gy_ref, sa_ref, sb_ref) = out_refs
    else:
        (kef_ref, keb_ref, v_ref, decf_ref, decb_ref, xc_ref) = out_refs
    tm = x_ref.shape[0]
    n_chunks = tm // GLA_CHUNK
    x = x_ref[...]
    shift = mod_ref[3:4, :]
    scale = mod_ref[4:5, :]
    u = (_rms(x) * nw_ref[...] * (1.0 + scale) + shift).astype(BF16)
    D = x.shape[1]
    W = wxl_ref.shape[1]
    CT = 2 * LANE
    assert tm % seg == 0 and seg % SUBLANE == 0 and cw_ref.shape[1] == seg

    def kc(h):
        return slice(h * GLA_DK, (h + 1) * GLA_DK)

    fd = _dot(u, wfd_ref[...]).astype(BF16)
    logits = _dot(fd, fup_ref[...]) + fb_ref[...]

    def xl_tile(j):
        return _dot(u, wxl_ref[:, j * CT:(j + 1) * CT])

    def qkv_tile(j):
        return _dot(u, wqkv_ref[:, j * CT:(j + 1) * CT])

    def rest_tile(j):
        return _dot(u, wrest_ref[:, j * CT:(j + 1) * CT])

    sub_pos = lax.broadcasted_iota(jnp.int32, (1, SUBLANE, 1), 1)
    vregs_per_chunk = GLA_CHUNK // SUBLANE

    def chunk_log_decay(r, h, d):
        lg = logits[r, d * GLA_KEY + h * GLA_DK:d * GLA_KEY + (h + 1) * GLA_DK]
        z = lg * LOG2E
        la = (jnp.minimum(z, 0.0) - jnp.log2(1.0 + jnp.exp2(-jnp.abs(z)))) * (1.0 / GLA_TAU)
        la = la.reshape(vregs_per_chunk, SUBLANE, GLA_DK)
        cs = la
        s = 1
        while s < SUBLANE:
            cs = cs + jnp.where(sub_pos >= s, pltpu.roll(cs, s, 1), 0.0)
            s *= 2
        last = cs[:, SUBLANE - 1:SUBLANE, :]
        carry = [jnp.zeros((1, 1, GLA_DK), F32)]
        for g in range(1, vregs_per_chunk):
            carry.append(carry[-1] + last[g - 1:g])
        tot = carry[-1] + last[vregs_per_chunk - 1:]
        return la, cs + jnp.concatenate(carry, axis=0), tot

    def conv(j, xl):
        cs = slice(j * CT, (j + 1) * CT)
        xc = cb_ref[:, cs][None]
        for t in range(CONV_K):
            off = t - CONV_K // 2
            term = xl if off == 0 else pltpu.roll(xl, (-off) % tm, 0)
            xc = xc + term.reshape(tm // seg, seg, CT) * cw_ref[t, :, cs][None]
        xc_ref[:, :, cs] = xc.reshape(tm // SUBLANE, SUBLANE, CT)

    tiles = {}

    def decayed_products(h, c):
        r = slice(c * GLA_CHUNK, (c + 1) * GLA_CHUNK)
        cols = slice((h % hpt) * GLA_DK, (h % hpt + 1) * GLA_DK)
        shape3 = (vregs_per_chunk, SUBLANE, GLA_DK)
        _, b_f, tot_f = chunk_log_decay(r, h, 0)
        la_b, cs_b, tot_b = chunk_log_decay(r, h, 1)
        pre_b = cs_b - la_b
        c_b = tot_b - pre_b
        e_f = tot_f - b_f
        decf_ref[c * SUBLANE:(c + 1) * SUBLANE, kc(h)] = jnp.broadcast_to(jnp.exp2(tot_f[0]), (SUBLANE, GLA_DK))
        decb_ref[c * SUBLANE:(c + 1) * SUBLANE, kc(h)] = jnp.broadcast_to(jnp.exp2(tot_b[0]), (SUBLANE, GLA_DK))

        def put(ref, val):
            ref[r, kc(h)] = val.reshape(GLA_CHUNK, GLA_DK).astype(BF16)

        k = tiles["k", h // hpt][r, cols].reshape(shape3)
        put(kef_ref, k * jnp.exp2(e_f))
        put(keb_ref, k * jnp.exp2(pre_b))
        if need_out:
            put(kif_ref, k * jnp.exp2(-b_f))
            put(kib_ref, k * jnp.exp2(-c_b))
            q = tiles["q", h // hpt][r, cols].reshape(shape3) * (GLA_DK ** -0.5)
            put(qdf_ref, q * jnp.exp2(b_f))
            put(qdb_ref, q * jnp.exp2(c_b))

    n_q = GLA_KEY // CT
    n_v = GLA_VAL // CT
    hpt = CT // GLA_DK
    assert GLA_KEY % CT == 0 and GLA_VAL % CT == 0 and W % CT == 0 and D % CT == 0

    mxu, pending = [], []

    def add_mxu(key, fn):
        mxu.append((key, fn))
        return len(mxu) - 1

    for j in range(W // CT):
        step = add_mxu(("xl", j), functools.partial(xl_tile, j))
        pending.append((step, 7, lambda j=j: conv(j, tiles.pop(("xl", j)))))
    last_qk = 0
    for j in range(n_q):
        last_qk = add_mxu(("k", j), functools.partial(qkv_tile, n_q + j))
    if need_out:
        for j in range(n_q):
            last_qk = add_mxu(("q", j), functools.partial(qkv_tile, j))
    for h in range(GLA_HEADS):
        for c in range(n_chunks):
            pending.append((last_qk, 3 if need_out else 2, functools.partial(decayed_products, h, c)))
    for j in range(n_v):
        step = add_mxu(("v", j), functools.partial(qkv_tile, 2 * n_q + j))

        def vcast(j=j):
            v_ref[:, j * CT:(j + 1) * CT] = tiles.pop(("v", j)).astype(BF16)

        pending.append((step, 1, vcast))
    if need_out:
        acts = ((sg_ref, lambda t: t * _sigmoid(t), 3), (gy_ref, _gelu_tanh, 6),
                (sa_ref, _sigmoid, 2), (sb_ref, _sigmoid, 2))
        per = D // CT
        assert GLA_VAL == D and W == D
        for j in range(len(acts) * per):
            step = add_mxu(("rest", j), functools.partial(rest_tile, j))
            o_ref_, fn, cost = acts[j // per]

            def act(j=j, o_ref_=o_ref_, fn=fn):
                o_ref_[:, (j % per) * CT:(j % per + 1) * CT] = fn(tiles.pop(("rest", j))).astype(BF16)

            pending.append((step, cost, act))

    bulk = [p for p in pending if p[2].__class__ is functools.partial]
    quick = [p for p in pending if p[2].__class__ is not functools.partial]
    quick.sort(key=lambda p: p[0])
    per_step = sum(c for _, c, _ in pending) / len(mxu)
    credit = 0.0
    for i, (key, fn) in enumerate(mxu):
        tiles[key] = fn()
        credit += per_step
        for queue in (quick, bulk):
            while queue and queue[0][0] < i and credit > 0:
                step, cost, task = queue.pop(0)
                task()
                credit -= cost
    for step, cost, task in quick + bulk:
        task()


def _segment_tap_weights(conv_w, seg):
    pos = jnp.arange(seg)[None, :, None]
    off = (jnp.arange(CONV_K) - CONV_K // 2)[:, None, None]
    inside = (pos + off >= 0) & (pos + off < seg)
    return jnp.where(inside, conv_w[:, None, :], 0.0)


def _mix_in(x2, mods, mod_index, nw, wts, *, B, L, tm, seg, need_out):
    rows, D = x2.shape
    W = wts["wxl"].shape[1]
    nt = L // tm
    grid = (B, nt)

    def tok(width):
        return pl.BlockSpec((tm, width), lambda b, j: (b * nt + j, 0))

    dec_spec = pl.BlockSpec((tm // SUBLANE, GLA_KEY), lambda b, j: (b * nt + j, 0))
    lru_spec = pl.BlockSpec((tm // SUBLANE, None, SUBLANE, W), lambda b, j: (j, b, 0, 0))
    tokbf = lambda width: jax.ShapeDtypeStruct((rows, width), BF16)
    dec_shape = jax.ShapeDtypeStruct((rows // SUBLANE, GLA_KEY), F32)
    lru_shape = jax.ShapeDtypeStruct((L // SUBLANE, B, SUBLANE, W), F32)

    n_key = 6 if need_out else 2
    out_shape = [tokbf(GLA_KEY)] * n_key + [tokbf(GLA_VAL), dec_shape, dec_shape, lru_shape]
    out_specs = [tok(GLA_KEY)] * n_key + [tok(GLA_VAL), dec_spec, dec_spec, lru_spec]
    if need_out:
        out_shape += [tokbf(GLA_VAL), tokbf(W), tokbf(D), tokbf(D)]
        out_specs += [tok(GLA_VAL), tok(W), tok(D), tok(D)]

    names = ["wqkv", "wfd", "fup", "fb", "wxl", "cw", "cb", "wrest"]
    w_list = [_segment_tap_weights(wts["cw"], seg) if n == "cw" else wts[n] for n in names]
    in_specs = [pl.BlockSpec((tm, D), lambda b, j: (b * nt + j, 0)),
                pl.BlockSpec((None, N_MOD, D), lambda b, j: (mod_index(b), 0, 0)),
                _const_spec((1, D))] + [_const_spec(w.shape) for w in w_list]
    kern = functools.partial(_mix_in_kernel, seg=seg, need_out=need_out)
    return pl.pallas_call(
        kern,
        out_shape=out_shape,
        grid=grid,
        in_specs=in_specs,
        out_specs=out_specs,
        compiler_params=_cparams("parallel", "parallel"),
        name="mix_in_out" if need_out else "mix_in_ctx",
    )(x2, mods, nw.reshape(1, D), *w_list)


def _lru_kernel(xf_ref, xb_ref, wg_ref, lam_ref, hf0_ref, hb0_ref, *refs, B, tt, need_out):
    if need_out:
        hf_ref, hb_ref, hfl_ref, hbl_ref, af_s, uf_s, ab_s, ub_s, sf_ref, sb_ref = refs
    else:
        hfl_ref, hbl_ref, af_s, uf_s, ab_s, ub_s, sf_ref, sb_ref = refs
    bw = xf_ref.shape[1]
    R = B * SUBLANE

    @pl.when(pl.program_id(1) == 0)
    def _():
        sf_ref[...] = hf0_ref[...]
        sb_ref[...] = hb0_ref[...]

    half_rate = (0.5 * LRU_C) * _softplus(-lam_ref[...])
    lane = lax.broadcasted_iota(jnp.int32, (R, bw), 1)
    ones = jnp.where(lane < 2, 1.0, 0.0).astype(BF16)

    def gates(x_ref, d, a_s, u_s, rows):
        xc = x_ref[rows, :]
        lhs = jnp.concatenate([xc.astype(BF16), ones], axis=1)
        z = _dot(lhs, wg_ref[:, 2 * d * bw:(2 * d + 2) * bw])
        t_r = jnp.tanh(z[:, :bw])
        t_i = jnp.tanh(z[:, bw:])
        hr = half_rate[d:d + 1, :]
        y = hr + hr * t_r
        a = jnp.exp2(y * (-LOG2E))
        a_s[rows, :] = a
        xh = 0.5 * xc
        u_s[rows, :] = _sqrt_nonneg(1.0 - a * a) * (xh + xh * t_i)

    hf = sf_ref[...]
    hb = sb_ref[...]
    n_oct = tt // SUBLANE
    for k in range(n_oct):
        kf, kb = k, n_oct - 1 - k
        gates(xf_ref, 0, af_s, uf_s, slice(kf * R, (kf + 1) * R))
        gates(xb_ref, 1, ab_s, ub_s, slice(kb * R, (kb + 1) * R))
        for t in range(SUBLANE):
            rf = pl.ds(kf * R + t, B, stride=SUBLANE)
            rb = pl.ds(kb * R + SUBLANE - 1 - t, B, stride=SUBLANE)
            hf = af_s[rf, :] * hf + uf_s[rf, :]
            hb = ab_s[rb, :] * hb + ub_s[rb, :]
            if need_out:
                hf_ref[rf, :] = hf
                hb_ref[rb, :] = hb
    sf_ref[...] = hf
    sb_ref[...] = hb
    hfl_ref[...] = hf
    hbl_ref[...] = hb


def _lru(xc, wg, lam, hf0, hb0, *, B, need_out):
    rows, W = xc.shape
    L = rows // B
    tt = SUBLANE * _pick_tile(L // SUBLANE, 32, 1)
    wb = LANE
    assert W // LRU_BLOCKS == wb
    nblk = L // tt
    fwd = pl.BlockSpec((tt * B, wb), lambda w, i: (i, w))
    bwd = pl.BlockSpec((tt * B, wb), lambda w, i: (nblk - 1 - i, w))
    st = pl.BlockSpec((B, wb), lambda w, i: (0, w))
    out_shape = [jax.ShapeDtypeStruct((B, W), F32)] * 2
    out_specs = [st, st]
    if need_out:
        out_shape = [jax.ShapeDtypeStruct((rows, W), F32)] * 2 + out_shape
        out_specs = [fwd, bwd] + out_specs
    kern = functools.partial(_lru_kernel, B=B, tt=tt, need_out=need_out)
    return pl.pallas_call(
        kern,
        out_shape=out_shape,
        grid=(W // wb, nblk),
        in_specs=[fwd, bwd,
                  pl.BlockSpec((None, 2 * wb, 4 * wb), lambda w, i: (w, 0, 0)),
                  pl.BlockSpec((2, wb), lambda w, i: (0, w)),
                  st, st],
        out_specs=out_specs,
        scratch_shapes=[pltpu.VMEM((tt * B, wb), F32)] * 4 + [pltpu.VMEM((B, wb), F32)] * 2,
        compiler_params=_cparams("parallel", "arbitrary"),
        name="lru_out" if need_out else "lru_ctx",
    )(xc, xc, wg, lam, hf0, hb0)


def _gla_kernel(ckef_ref, ckeb_ref, cv_ref, cdf_ref, cdb_ref,
                qdf_ref, kef_ref, vf_ref, df_ref,
                qdb_ref, keb_ref, vb_ref, db_ref,
                of_ref, ob_ref, st_ref):
    C = GLA_CHUNK
    H = GLA_HEADS
    nc = cv_ref.shape[0] // C
    n = vf_ref.shape[0] // C

    def rows(c):
        return pl.ds(pl.multiple_of(c * C, C), C)

    def dec_rows(c):
        return pl.ds(pl.multiple_of(c * SUBLANE, SUBLANE), SUBLANE)

    def kc(h):
        return slice(h * GLA_DK, (h + 1) * GLA_DK)

    def vc(h):
        return slice(h * GLA_DV, (h + 1) * GLA_DV)

    def advance(idx, ke, v, dec8):
        st_ref[idx] = st_ref[idx] * dec8[0:1, :] + _dot_tn(v, ke)

    def attend(idx, qd):
        return _dot_nt(qd, st_ref[idx].astype(BF16)).astype(BF16)

    i = pl.program_id(1)

    @pl.when(i == 0)
    def _():
        st_ref[...] = jnp.zeros(st_ref.shape, F32)

        def ctx_step(s, carry):
            rf, rb = rows(s), rows(nc - 1 - s)
            df8, db8 = cdf_ref[dec_rows(s), :], cdb_ref[dec_rows(nc - 1 - s), :]
            for h in range(H):
                advance(2 * h, ckef_ref[rf, kc(h)], cv_ref[rf, vc(h)], df8[:, kc(h)])
                advance(2 * h + 1, ckeb_ref[rb, kc(h)], cv_ref[rb, vc(h)], db8[:, kc(h)])
            return carry

        lax.fori_loop(0, nc, ctx_step, 0, unroll=2)

    @pl.when(i > 0)
    def _():
        def lat_step(s, carry):
            rf, rb = rows(s), rows(n - 1 - s)
            df8, db8 = df_ref[dec_rows(s), :], db_ref[dec_rows(n - 1 - s), :]
            for h in range(H):
                of_ref[rf, vc(h)] = attend(2 * h, qdf_ref[rf, kc(h)])
                advance(2 * h, kef_ref[rf, kc(h)], vf_ref[rf, vc(h)], df8[:, kc(h)])
                ob_ref[rb, vc(h)] = attend(2 * h + 1, qdb_ref[rb, kc(h)])
                advance(2 * h + 1, keb_ref[rb, kc(h)], vb_ref[rb, vc(h)], db8[:, kc(h)])
            return carry

        lax.fori_loop(0, n, lat_step, 0, unroll=4)


def _gla(ctx_t, lat_t, *, B, Tc, T, tl):
    ckef, ckeb, cv, cdf, cdb = ctx_t
    qdf, kef, qdb, keb, v, df, db = lat_t
    nt = T // tl

    def cblk(rows_, width):
        return pl.BlockSpec((rows_, width), lambda b, i: (b, 0))

    def fblk(rows_, width):
        return pl.BlockSpec((rows_, width), lambda b, i: (b * nt + jnp.maximum(i - 1, 0), 0))

    def bblk(rows_, width):
        return pl.BlockSpec((rows_, width), lambda b, i: (b * nt + nt - 1 - jnp.maximum(i - 1, 0), 0))

    o_shape = jax.ShapeDtypeStruct((B * T, GLA_VAL), BF16)
    return pl.pallas_call(
        _gla_kernel,
        out_shape=[o_shape, o_shape],
        grid=(B, 1 + nt),
        in_specs=[cblk(Tc, GLA_KEY), cblk(Tc, GLA_KEY), cblk(Tc, GLA_VAL),
                  cblk(Tc // SUBLANE, GLA_KEY), cblk(Tc // SUBLANE, GLA_KEY),
                  fblk(tl, GLA_KEY), fblk(tl, GLA_KEY), fblk(tl, GLA_VAL), fblk(tl // SUBLANE, GLA_KEY),
                  bblk(tl, GLA_KEY), bblk(tl, GLA_KEY), bblk(tl, GLA_VAL), bblk(tl // SUBLANE, GLA_KEY)],
        out_specs=[fblk(tl, GLA_VAL), bblk(tl, GLA_VAL)],
        scratch_shapes=[pltpu.VMEM((2 * GLA_HEADS, GLA_DV, GLA_DK), F32)],
        compiler_params=_cparams("parallel", "arbitrary"),
        name="gla",
    )(ckef, ckeb, cv, cdf, cdb, qdf, kef, v, df, qdb, keb, v, db)


def _mix_out_kernel(x_ref, mod_ref, of_ref, ob_ref, qdf_ref, kif_ref, qdb_ref, kib_ref, v_ref,
                    sg_ref, gnw_ref, hf_ref, hb_ref, gy_ref, sa_ref, sb_ref,
                    wgla_ref, wlru_ref, wo_ref, o_ref, oi_ref):
    tm, W = gy_ref.shape
    hsum = (hf_ref[...] + hb_ref[...]).reshape(tm, W)
    hl = (hsum * gy_ref[...].astype(F32)).astype(BF16)
    y_lru = _dot(hl, wlru_ref[...])
    R = _pick_tile(tm, 2 * LANE, GLA_CHUNK)
    ri = lax.broadcasted_iota(jnp.int32, (R, R), 0)
    ci = lax.broadcasted_iota(jnp.int32, (R, R), 1)
    same = (ri // GLA_CHUNK) == (ci // GLA_CHUNK)
    fwd_ok = same & (ci <= ri)
    bwd_ok = same & (ci > ri)
    og = []
    for h in range(GLA_HEADS):
        ks = slice(h * GLA_DK, (h + 1) * GLA_DK)
        cs = slice(h * GLA_DV, (h + 1) * GLA_DV)
        for c in range(tm // R):
            r = slice(c * R, (c + 1) * R)
            sf = _dot_nt(qdf_ref[r, ks], kif_ref[r, ks])
            sb = _dot_nt(qdb_ref[r, ks], kib_ref[r, ks])
            sc = jnp.where(fwd_ok, sf, jnp.where(bwd_ok, sb, 0.0))
            oi_ref[r, cs] = _dot(sc.astype(BF16), v_ref[r, cs])
        o = oi_ref[:, cs] + of_ref[:, cs].astype(F32) + ob_ref[:, cs].astype(F32)
        og.append((_rms(o) * gnw_ref[...] * sg_ref[:, cs].astype(F32)).astype(BF16))
    y_gla = _dot(jnp.concatenate(og, axis=1), wgla_ref[...])
    merged = sa_ref[...].astype(F32) * y_gla + sb_ref[...].astype(F32) * y_lru
    y = _dot(merged.astype(BF16), wo_ref[...])
    o_ref[...] = x_ref[...] + mod_ref[5:6, :] * y


def _mix_out(x2, mods, of, ob, qdf, kif, qdb, kib, v, sg, gnw, hf, hb, gy, sa, sb, wgla, wlru, wo, *, B, T, tm):
    rows, D = x2.shape
    W = wlru.shape[0]
    nt = T // tm
    tok = lambda width: pl.BlockSpec((tm, width), lambda b, j: (b * nt + j, 0))
    lru = pl.BlockSpec((tm // SUBLANE, None, SUBLANE, W), lambda b, j: (j, b, 0, 0))
    return pl.pallas_call(
        _mix_out_kernel,
        out_shape=jax.ShapeDtypeStruct((rows, D), F32),
        grid=(B, nt),
        in_specs=[tok(D), pl.BlockSpec((None, N_MOD, D), lambda b, j: (b, 0, 0)),
                  tok(GLA_VAL), tok(GLA_VAL),
                  tok(GLA_KEY), tok(GLA_KEY), tok(GLA_KEY), tok(GLA_KEY), tok(GLA_VAL),
                  tok(GLA_VAL), _const_spec((1, GLA_DV)),
                  lru, lru, tok(W), tok(D), tok(D),
                  _const_spec(wgla.shape), _const_spec(wlru.shape), _const_spec(wo.shape)],
        out_specs=tok(D),
        scratch_shapes=[pltpu.VMEM((tm, GLA_VAL), F32)],
        compiler_params=_cparams("parallel", "parallel"),
        name="mix_out",
    )(x2, mods, of, ob, qdf, kif, qdb, kib, v, sg, gnw.reshape(1, GLA_DV), hf, hb, gy, sa, sb, wgla, wlru, wo)


def _prep_mixer_weights(w_in, fup, fb, conv_w, conv_b, wr, br, wi, bi, lam):
    D = w_in.shape[0]
    o = 0
    wqkv = w_in[:, o:o + 2 * GLA_KEY + GLA_VAL]
    o += 2 * GLA_KEY + GLA_VAL
    w_g = w_in[:, o:o + GLA_VAL]
    o += GLA_VAL
    wfd = w_in[:, o:o + 2 * GLA_RANK]
    o += 2 * GLA_RANK
    W = lam.shape[1]
    wxl = w_in[:, o:o + W]
    o += W
    w_tail = w_in[:, o:]
    wfd_p = jnp.zeros((D, LANE), F32).at[:, :2 * GLA_RANK].set(wfd)
    fup_p = jnp.zeros((LANE, 2 * GLA_KEY), F32)
    fup_p = fup_p.at[:GLA_RANK, :GLA_KEY].set(fup[0]).at[GLA_RANK:2 * GLA_RANK, GLA_KEY:].set(fup[1])
    wg = 0.5 * jnp.concatenate([wr[0], wi[0], wr[1], wi[1]], axis=-1)
    bw = W // LRU_BLOCKS
    bg = 0.5 * jnp.stack([br[0], bi[0], br[1], bi[1]])
    bg = bg.reshape(4, LRU_BLOCKS, bw).transpose(1, 0, 2).reshape(LRU_BLOCKS, 1, 4 * bw)
    bg_hi = bg.astype(BF16)
    bg_lo = (bg - bg_hi.astype(F32)).astype(BF16)
    wg_aug = jnp.concatenate([wg.astype(BF16), bg_hi, bg_lo,
                              jnp.zeros((LRU_BLOCKS, bw - 2, 4 * bw), BF16)], axis=1)
    return {
        "wqkv": wqkv.astype(BF16),
        "wfd": wfd_p.astype(BF16),
        "fup": fup_p.astype(BF16),
        "fb": jnp.concatenate([fb[0], fb[1]]).reshape(1, 2 * GLA_KEY),
        "wxl": wxl.astype(BF16),
        "cw": conv_w,
        "cb": conv_b.reshape(1, W),
        "wg": wg_aug,
        "lam": lam,
        "wrest": jnp.concatenate([w_g, w_tail], axis=1).astype(BF16),
    }


def kernel(x, c, ctx, c_ctx, w_ada, b_ada, norm_w, ffn1_wi, ffn1_wo, ffn2_wi, ffn2_wo, w_in, gla_fup, gla_fb, gla_norm_w, conv_w, conv_b, lru_wr, lru_br, lru_wi, lru_bi, lru_lam, w_out_gla, w_out_lru, w_o, final_norm_w):
    B, T, D = x.shape
    Tc = ctx.shape[1]
    depth = w_ada.shape[0]
    assert depth == 1, "single trunk layer"
    assert B % SUBLANE == 0 and T % GRID_W == 0 and Tc % GLA_CHUNK == 0
    l = 0

    R = -(-(B + 1) // SUBLANE) * SUBLANE
    cs = jnp.zeros((R, D), F32).at[:B].set(c).at[B].set(c_ctx)
    mods = _ada(cs, w_ada, b_ada, l).reshape(R, N_MOD, D)

    x2 = x.reshape(B * T, D)
    c2 = ctx.reshape(B * Tc, D)
    tm_lat = _pick_tile(T, 512, GRID_W)
    wi1, wo1 = ffn1_wi[l].astype(BF16), ffn1_wo[l].astype(BF16)
    wi2, wo2 = ffn2_wi[l].astype(BF16), ffn2_wo[l].astype(BF16)
    tm_ffn = _pick_tile(T, 1024, GRID_W)
    nt_ffn = T // tm_ffn
    tm_cf = _pick_tile(B * Tc, 1024, SUBLANE)

    h = _ffn(x2, mods, lambda i: i // nt_ffn, norm_w[l, 0], wi1, wo1, final_norm_w,
             sub=0, tm=tm_ffn, final_norm=False)
    hc = _ffn(c2, mods, lambda i: B, norm_w[l, 0], wi1, wo1, final_norm_w,
              sub=0, tm=tm_cf, final_norm=False)

    wts = _prep_mixer_weights(w_in[l], gla_fup[l], gla_fb[l], conv_w[l], conv_b[l],
                              lru_wr[l], lru_br[l], lru_wi[l], lru_bi[l], lru_lam[l])
    W = lru_lam.shape[-1]
    ctx_o = _mix_in(hc, mods, lambda b: B, norm_w[l, 1], wts,
                    B=B, L=Tc, tm=Tc, seg=Tc, need_out=False)
    ckef, ckeb, cv, cdf, cdb, cxc = ctx_o
    lat_o = _mix_in(h, mods, lambda b: b, norm_w[l, 1], wts,
                    B=B, L=T, tm=tm_lat, seg=GRID_W, need_out=True)
    (qdf, kif, kef, qdb, kib, keb, v, df, db, xc, sg, gy, sa, sb) = lat_o

    zero = jnp.zeros((B, W), F32)
    hf_c, hb_c = _lru(cxc.reshape(Tc * B, W), wts["wg"], wts["lam"], zero, zero,
                      B=B, need_out=False)
    hf, hb, _, _ = _lru(xc.reshape(T * B, W), wts["wg"], wts["lam"], hf_c, hb_c,
                        B=B, need_out=True)

    of, ob = _gla((ckef, ckeb, cv, cdf, cdb), (qdf, kef, qdb, keb, v, df, db),
                  B=B, Tc=Tc, T=T, tl=_pick_tile(T, 1024, GLA_CHUNK))

    oct_ = lambda t: t.reshape(T // SUBLANE, B, SUBLANE, W)
    h = _mix_out(h, mods, of, ob, qdf, kif, qdb, kib, v, sg, gla_norm_w[l], oct_(hf), oct_(hb), gy, sa, sb,
                 w_out_gla[l].astype(BF16), w_out_lru[l].astype(BF16), w_o[l].astype(BF16),
                 B=B, T=T, tm=tm_lat)

    out = _ffn(h, mods, lambda i: i // nt_ffn, norm_w[l, 2], wi2, wo2, final_norm_w,
               sub=2, tm=tm_ffn, final_norm=True)
    return out.reshape(B, T, D)
```

```python
import functools

import jax
import jax.numpy as jnp
from jax import lax
from jax.experimental import pallas as pl
from jax.experimental.pallas import tpu as pltpu

F32 = jnp.float32
BF16 = jnp.bfloat16

N_MOD = 9
GLA_HEADS = 4
GLA_DK = 128
GLA_DV = 256
GLA_KEY = GLA_HEADS * GLA_DK
GLA_VAL = GLA_HEADS * GLA_DV
GLA_RANK = 16
GLA_TAU = 16.0
GLA_CHUNK = 64
GRID_W = 64
LRU_BLOCKS = 8
LRU_C = 8.0
CONV_K = 4
EPS = 1e-6

LANE = 128
SUBLANE = 8
VMEM_LIMIT = 56 * 1024 * 1024


def _cparams(*sem):
    return pltpu.CompilerParams(dimension_semantics=sem, vmem_limit_bytes=VMEM_LIMIT)


def _const_spec(shape):
    nd = len(shape)
    return pl.BlockSpec(shape, lambda *_: (0,) * nd, pipeline_mode=pl.Buffered(1))


def _dot(a, b):
    return jnp.dot(a, b, preferred_element_type=F32)


def _dot_tn(a, b):
    return lax.dot_general(a, b, (((0,), (0,)), ((), ())), preferred_element_type=F32)


def _dot_nt(a, b):
    return lax.dot_general(a, b, (((1,), (1,)), ((), ())), preferred_element_type=F32)


LOG2E = 1.4426950408889634
F32_TINY = 1.1754943508222875e-38


def _sigmoid(x):
    return 1.0 / (1.0 + jnp.exp2(x * (-LOG2E)))


def _sqrt_nonneg(x):
    return x * lax.rsqrt(jnp.maximum(x, F32_TINY))


def _softplus(x):
    return jnp.maximum(x, 0.0) + jnp.log1p(jnp.exp(-jnp.abs(x)))


def _gelu_tanh(x):
    c = 0.7978845608028654
    hx = 0.5 * x
    return hx + hx * jnp.tanh(x * (c + (c * 0.044715) * (x * x)))


def _rms(x):
    return x * lax.rsqrt(jnp.mean(x * x, axis=-1, keepdims=True) + EPS)


def _pick_tile(n, target, quantum):
    best = None
    t = quantum
    while t <= min(n, target):
        if n % t == 0:
            best = t
        t += quantum
    assert best is not None, (n, target, quantum)
    return best


def _ada_kernel(c_ref, w_ref, b_ref, o_ref):
    c = c_ref[...]
    s = (c * _sigmoid(c)).astype(BF16)
    o_ref[...] = _dot(s, w_ref[...].astype(BF16)) + b_ref[...]


def _ada(cs, w_ada, b_ada, layer):
    R, D = cs.shape
    N = w_ada.shape[2]
    tn = _pick_tile(N, 2304, LANE)
    return pl.pallas_call(
        _ada_kernel,
        out_shape=jax.ShapeDtypeStruct((R, N), F32),
        grid=(N // tn,),
        in_specs=[pl.BlockSpec((R, D), lambda j: (0, 0)),
                  pl.BlockSpec((None, D, tn), lambda j: (layer, 0, j)),
                  pl.BlockSpec((1, tn), lambda j: (0, j))],
        out_specs=pl.BlockSpec((R, tn), lambda j: (0, j)),
        compiler_params=_cparams("parallel"),
        name="ada",
    )(cs, w_ada, b_ada[layer].reshape(1, N))


def _ffn_kernel(x_ref, *refs, sub, tf, final_norm, branch_sub):
    if branch_sub is None:
        mod_ref, nw_ref, wi_ref, wo_ref, fnw_ref, o_ref, a_ref = refs
        x = x_ref[...]
    else:
        y_ref, mod_ref, nw_ref, wi_ref, wo_ref, fnw_ref, o_ref, a_ref = refs
        x = x_ref[...] + mod_ref[3 * branch_sub + 2:3 * branch_sub + 3, :] * y_ref[...].astype(F32)
    F = wo_ref.shape[0]
    shift = mod_ref[3 * sub:3 * sub + 1, :]
    scale = mod_ref[3 * sub + 1:3 * sub + 2, :]
    gate = mod_ref[3 * sub + 2:3 * sub + 3, :]
    u = (_rms(x) * nw_ref[...] * (1.0 + scale) + shift).astype(BF16)
    for j in range(F // tf):
        g = _dot(u, wi_ref[:, j * tf:(j + 1) * tf])
        up = _dot(u, wi_ref[:, F + j * tf:F + (j + 1) * tf])
        a_ref[:, j * tf:(j + 1) * tf] = (g * _sigmoid(g) * up).astype(BF16)
    acc = _dot(a_ref[...], wo_ref[...])
    out = x + (0.5 * gate) * acc
    if final_norm:
        out = _rms(out) * fnw_ref[...]
    o_ref[...] = out


def _ffn(x2, mods, mod_index, nw, wi, wo, fnw, *, sub, tm, final_norm, branch=None, branch_sub=None):
    rows, D = x2.shape
    F = wo.shape[0]
    tf = 2 * LANE
    assert F % tf == 0
    assert (branch is None) == (branch_sub is None)
    kern = functools.partial(_ffn_kernel, sub=sub, tf=tf, final_norm=final_norm, branch_sub=branch_sub)
    tok = pl.BlockSpec((tm, D), lambda i: (i, 0))
    streams = [x2] if branch is None else [x2, branch]
    return pl.pallas_call(
        kern,
        out_shape=jax.ShapeDtypeStruct((rows, D), F32),
        scratch_shapes=[pltpu.VMEM((tm, F), BF16)],
        grid=(rows // tm,),
        in_specs=[tok] * len(streams) + [
                  pl.BlockSpec((None, N_MOD, D), lambda i: (mod_index(i), 0, 0)),
                  _const_spec((1, D)),
                  _const_spec((D, 2 * F)),
                  _const_spec((F, D)),
                  _const_spec((1, D))],
        out_specs=tok,
        compiler_params=_cparams("parallel"),
        name="ffn_sub%d" % sub,
    )(*streams, mods, nw.reshape(1, D), wi, wo, fnw.reshape(1, D))


def _mix_in_kernel(x_ref, mod_ref, nw_ref, wqkv_ref, wfd_ref, fup_ref, fb_ref, wxl_ref,
                   cw_ref, cb_ref, wrest_ref, *out_refs, seg, need_out):
    if need_out:
        (qdf_ref, kif_ref, kef_ref, qdb_ref, kib_ref, keb_ref, v_ref, decf_ref, decb_ref,
         xc_ref, sg_ref, gy_ref, sa_ref, sb_ref) = out_refs
    else:
        (kef_ref, keb_ref, v_ref, decf_ref, decb_ref, xc_ref) = out_refs
    tm = x_ref.shape[0]
    n_chunks = tm // GLA_CHUNK
    x = x_ref[...]
    shift = mod_ref[3:4, :]
    scale = mod_ref[4:5, :]
    u = (_rms(x) * nw_ref[...] * (1.0 + scale) + shift).astype(BF16)
    D = x.shape[1]
    W = wxl_ref.shape[1]
    CT = 2 * LANE
    assert tm % seg == 0 and seg % SUBLANE == 0 and cw_ref.shape[1] == seg

    def kc(h):
        return slice(h * GLA_DK, (h + 1) * GLA_DK)

    fd = _dot(u, wfd_ref[...]).astype(BF16)
    logits = _dot(fd, fup_ref[...]) + fb_ref[...]

    def xl_tile(j):
        return _dot(u, wxl_ref[:, j * CT:(j + 1) * CT])

    def qkv_tile(j):
        return _dot(u, wqkv_ref[:, j * CT:(j + 1) * CT])

    def rest_tile(j):
        return _dot(u, wrest_ref[:, j * CT:(j + 1) * CT])

    sub_pos = lax.broadcasted_iota(jnp.int32, (1, SUBLANE, 1), 1)
    vregs_per_chunk = GLA_CHUNK // SUBLANE

    def chunk_log_decay(r, h, d):
        lg = logits[r, d * GLA_KEY + h * GLA_DK:d * GLA_KEY + (h + 1) * GLA_DK]
        z = lg * LOG2E
        la = (jnp.minimum(z, 0.0) - jnp.log2(1.0 + jnp.exp2(-jnp.abs(z)))) * (1.0 / GLA_TAU)
        la = la.reshape(vregs_per_chunk, SUBLANE, GLA_DK)
        cs = la
        s = 1
        while s < SUBLANE:
            cs = cs + jnp.where(sub_pos >= s, pltpu.roll(cs, s, 1), 0.0)
            s *= 2
        last = cs[:, SUBLANE - 1:SUBLANE, :]
        carry = [jnp.zeros((1, 1, GLA_DK), F32)]
        for g in range(1, vregs_per_chunk):
            carry.append(carry[-1] + last[g - 1:g])
        tot = carry[-1] + last[vregs_per_chunk - 1:]
        return la, cs + jnp.concatenate(carry, axis=0), tot

    def conv(j, xl):
        cs = slice(j * CT, (j + 1) * CT)
        xc = cb_ref[:, cs][None]
        for t in range(CONV_K):
            off = t - CONV_K // 2
            term = xl if off == 0 else pltpu.roll(xl, (-off) % tm, 0)
            xc = xc + term.reshape(tm // seg, seg, CT) * cw_ref[t, :, cs][None]
        xc_ref[:, :, cs] = xc.reshape(tm // SUBLANE, SUBLANE, CT)

    tiles = {}

    def decayed_products(h, c):
        r = slice(c * GLA_CHUNK, (c + 1) * GLA_CHUNK)
        cols = slice((h % hpt) * GLA_DK, (h % hpt + 1) * GLA_DK)
        shape3 = (vregs_per_chunk, SUBLANE, GLA_DK)
        _, b_f, tot_f = chunk_log_decay(r, h, 0)
        la_b, cs_b, tot_b = chunk_log_decay(r, h, 1)
        pre_b = cs_b - la_b
        c_b = tot_b - pre_b
        e_f = tot_f - b_f
        decf_ref[c * SUBLANE:(c + 1) * SUBLANE, kc(h)] = jnp.broadcast_to(jnp.exp2(tot_f[0]), (SUBLANE, GLA_DK))
        decb_ref[c * SUBLANE:(c + 1) * SUBLANE, kc(h)] = jnp.broadcast_to(jnp.exp2(tot_b[0]), (SUBLANE, GLA_DK))

        def put(ref, val):
            ref[r, kc(h)] = val.reshape(GLA_CHUNK, GLA_DK).astype(BF16)

        k = tiles["k", h // hpt][r, cols].reshape(shape3)
        put(kef_ref, k * jnp.exp2(e_f))
        put(keb_ref, k * jnp.exp2(pre_b))
        if need_out:
            put(kif_ref, k * jnp.exp2(-b_f))
            put(kib_ref, k * jnp.exp2(-c_b))
            q = tiles["q", h // hpt][r, cols].reshape(shape3) * (GLA_DK ** -0.5)
            put(qdf_ref, q * jnp.exp2(b_f))
            put(qdb_ref, q * jnp.exp2(c_b))

    n_q = GLA_KEY // CT
    n_v = GLA_VAL // CT
    hpt = CT // GLA_DK
    assert GLA_KEY % CT == 0 and GLA_VAL % CT == 0 and W % CT == 0 and D % CT == 0

    mxu, pending = [], []

    def add_mxu(key, fn):
        mxu.append((key, fn))
        return len(mxu) - 1

    for j in range(W // CT):
        step = add_mxu(("xl", j), functools.partial(xl_tile, j))
        pending.append((step, 7, lambda j=j: conv(j, tiles.pop(("xl", j)))))
    last_qk = 0
    for j in range(n_q):
        last_qk = add_mxu(("k", j), functools.partial(qkv_tile, n_q + j))
    if need_out:
        for j in range(n_q):
            last_qk = add_mxu(("q", j), functools.partial(qkv_tile, j))
    for h in range(GLA_HEADS):
        for c in range(n_chunks):
            pending.append((last_qk, 3 if need_out else 2, functools.partial(decayed_products, h, c)))
    for j in range(n_v):
        step = add_mxu(("v", j), functools.partial(qkv_tile, 2 * n_q + j))

        def vcast(j=j):
            v_ref[:, j * CT:(j + 1) * CT] = tiles.pop(("v", j)).astype(BF16)

        pending.append((step, 1, vcast))
    if need_out:
        acts = ((sg_ref, lambda t: t * _sigmoid(t), 3), (gy_ref, _gelu_tanh, 6),
                (sa_ref, _sigmoid, 2), (sb_ref, _sigmoid, 2))
        per = D // CT
        assert GLA_VAL == D and W == D
        for j in range(len(acts) * per):
            step = add_mxu(("rest", j), functools.partial(rest_tile, j))
            o_ref_, fn, cost = acts[j // per]

            def act(j=j, o_ref_=o_ref_, fn=fn):
                o_ref_[:, (j % per) * CT:(j % per + 1) * CT] = fn(tiles.pop(("rest", j))).astype(BF16)

            pending.append((step, cost, act))

    bulk = [p for p in pending if p[2].__class__ is functools.partial]
    quick = [p for p in pending if p[2].__class__ is not functools.partial]
    quick.sort(key=lambda p: p[0])
    per_step = sum(c for _, c, _ in pending) / len(mxu)
    credit = 0.0
    for i, (key, fn) in enumerate(mxu):
        tiles[key] = fn()
        credit += per_step
        for queue in (quick, bulk):
            while queue and queue[0][0] < i and credit > 0:
                step, cost, task = queue.pop(0)
                task()
                credit -= cost
    for step, cost, task in quick + bulk:
        task()


def _segment_tap_weights(conv_w, seg):
    pos = jnp.arange(seg)[None, :, None]
    off = (jnp.arange(CONV_K) - CONV_K // 2)[:, None, None]
    inside = (pos + off >= 0) & (pos + off < seg)
    return jnp.where(inside, conv_w[:, None, :], 0.0)


def _mix_in(x2, mods, mod_index, nw, wts, *, B, L, tm, seg, need_out):
    rows, D = x2.shape
    W = wts["wxl"].shape[1]
    nt = L // tm
    grid = (B, nt)

    def tok(width):
        return pl.BlockSpec((tm, width), lambda b, j: (b * nt + j, 0))

    dec_spec = pl.BlockSpec((tm // SUBLANE, GLA_KEY), lambda b, j: (b * nt + j, 0))
    lru_spec = pl.BlockSpec((tm // SUBLANE, None, SUBLANE, W), lambda b, j: (j, b, 0, 0))
    tokbf = lambda width: jax.ShapeDtypeStruct((rows, width), BF16)
    dec_shape = jax.ShapeDtypeStruct((rows // SUBLANE, GLA_KEY), F32)
    lru_shape = jax.ShapeDtypeStruct((L // SUBLANE, B, SUBLANE, W), F32)

    n_key = 6 if need_out else 2
    out_shape = [tokbf(GLA_KEY)] * n_key + [tokbf(GLA_VAL), dec_shape, dec_shape, lru_shape]
    out_specs = [tok(GLA_KEY)] * n_key + [tok(GLA_VAL), dec_spec, dec_spec, lru_spec]
    if need_out:
        out_shape += [tokbf(GLA_VAL), tokbf(W), tokbf(D), tokbf(D)]
        out_specs += [tok(GLA_VAL), tok(W), tok(D), tok(D)]

    names = ["wqkv", "wfd", "fup", "fb", "wxl", "cw", "cb", "wrest"]
    w_list = [_segment_tap_weights(wts["cw"], seg) if n == "cw" else wts[n] for n in names]
    in_specs = [pl.BlockSpec((tm, D), lambda b, j: (b * nt + j, 0)),
                pl.BlockSpec((None, N_MOD, D), lambda b, j: (mod_index(b), 0, 0)),
                _const_spec((1, D))] + [_const_spec(w.shape) for w in w_list]
    kern = functools.partial(_mix_in_kernel, seg=seg, need_out=need_out)
    return pl.pallas_call(
        kern,
        out_shape=out_shape,
        grid=grid,
        in_specs=in_specs,
        out_specs=out_specs,
        compiler_params=_cparams("parallel", "parallel"),
        name="mix_in_out" if need_out else "mix_in_ctx",
    )(x2, mods, nw.reshape(1, D), *w_list)


def _lru_kernel(xf_ref, xb_ref, wg_ref, lam_ref, hf0_ref, hb0_ref, *refs, B, tt, need_out):
    if need_out:
        hf_ref, hb_ref, hfl_ref, hbl_ref, af_s, uf_s, ab_s, ub_s, sf_ref, sb_ref = refs
    else:
        hfl_ref, hbl_ref, af_s, uf_s, ab_s, ub_s, sf_ref, sb_ref = refs
    bw = xf_ref.shape[1]
    R = B * SUBLANE

    @pl.when(pl.program_id(1) == 0)
    def _():
        sf_ref[...] = hf0_ref[...]
        sb_ref[...] = hb0_ref[...]

    half_rate = (0.5 * LRU_C) * _softplus(-lam_ref[...])
    lane = lax.broadcasted_iota(jnp.int32, (R, bw), 1)
    ones = jnp.where(lane < 2, 1.0, 0.0).astype(BF16)

    def gates(x_ref, d, a_s, u_s, rows):
        xc = x_ref[rows, :]
        lhs = jnp.concatenate([xc.astype(BF16), ones], axis=1)
        z = _dot(lhs, wg_ref[:, 2 * d * bw:(2 * d + 2) * bw])
        t_r = jnp.tanh(z[:, :bw])
        t_i = jnp.tanh(z[:, bw:])
        hr = half_rate[d:d + 1, :]
        y = hr + hr * t_r
        a = jnp.exp2(y * (-LOG2E))
        a_s[rows, :] = a
        xh = 0.5 * xc
        u_s[rows, :] = _sqrt_nonneg(1.0 - a * a) * (xh + xh * t_i)

    hf = sf_ref[...]
    hb = sb_ref[...]
    n_oct = tt // SUBLANE
    for k in range(n_oct):
        kf, kb = k, n_oct - 1 - k
        gates(xf_ref, 0, af_s, uf_s, slice(kf * R, (kf + 1) * R))
        gates(xb_ref, 1, ab_s, ub_s, slice(kb * R, (kb + 1) * R))
        for t in range(SUBLANE):
            rf = pl.ds(kf * R + t, B, stride=SUBLANE)
            rb = pl.ds(kb * R + SUBLANE - 1 - t, B, stride=SUBLANE)
            hf = af_s[rf, :] * hf + uf_s[rf, :]
            hb = ab_s[rb, :] * hb + ub_s[rb, :]
            if need_out:
                hf_ref[rf, :] = hf
                hb_ref[rb, :] = hb
    sf_ref[...] = hf
    sb_ref[...] = hb
    hfl_ref[...] = hf
    hbl_ref[...] = hb


def _lru(xc, wg, lam, hf0, hb0, *, B, need_out):
    rows, W = xc.shape
    L = rows // B
    tt = SUBLANE * _pick_tile(L // SUBLANE, 32, 1)
    wb = LANE
    assert W // LRU_BLOCKS == wb
    nblk = L // tt
    fwd = pl.BlockSpec((tt * B, wb), lambda w, i: (i, w))
    bwd = pl.BlockSpec((tt * B, wb), lambda w, i: (nblk - 1 - i, w))
    st = pl.BlockSpec((B, wb), lambda w, i: (0, w))
    out_shape = [jax.ShapeDtypeStruct((B, W), F32)] * 2
    out_specs = [st, st]
    if need_out:
        out_shape = [jax.ShapeDtypeStruct((rows, W), F32)] * 2 + out_shape
        out_specs = [fwd, bwd] + out_specs
    kern = functools.partial(_lru_kernel, B=B, tt=tt, need_out=need_out)
    return pl.pallas_call(
        kern,
        out_shape=out_shape,
        grid=(W // wb, nblk),
        in_specs=[fwd, bwd,
                  pl.BlockSpec((None, 2 * wb, 4 * wb), lambda w, i: (w, 0, 0)),
                  pl.BlockSpec((2, wb), lambda w, i: (0, w)),
                  st, st],
        out_specs=out_specs,
        scratch_shapes=[pltpu.VMEM((tt * B, wb), F32)] * 4 + [pltpu.VMEM((B, wb), F32)] * 2,
        compiler_params=_cparams("parallel", "arbitrary"),
        name="lru_out" if need_out else "lru_ctx",
    )(xc, xc, wg, lam, hf0, hb0)


def _gla_kernel(ckef_ref, ckeb_ref, cv_ref, cdf_ref, cdb_ref,
                qdf_ref, kef_ref, vf_ref, df_ref,
                qdb_ref, keb_ref, vb_ref, db_ref,
                of_ref, ob_ref, st_ref):
    C = GLA_CHUNK
    H = GLA_HEADS
    nc = cv_ref.shape[0] // C
    n = vf_ref.shape[0] // C

    def rows(c):
        return pl.ds(pl.multiple_of(c * C, C), C)

    def dec_rows(c):
        return pl.ds(pl.multiple_of(c * SUBLANE, SUBLANE), SUBLANE)

    def kc(h):
        return slice(h * GLA_DK, (h + 1) * GLA_DK)

    def vc(h):
        return slice(h * GLA_DV, (h + 1) * GLA_DV)

    def advance(idx, ke, v, dec8):
        st_ref[idx] = st_ref[idx] * dec8[0:1, :] + _dot_tn(v, ke)

    def attend(idx, qd):
        return _dot_nt(qd, st_ref[idx].astype(BF16)).astype(BF16)

    i = pl.program_id(1)

    @pl.when(i == 0)
    def _():
        st_ref[...] = jnp.zeros(st_ref.shape, F32)

        def ctx_step(s, carry):
            rf, rb = rows(s), rows(nc - 1 - s)
            df8, db8 = cdf_ref[dec_rows(s), :], cdb_ref[dec_rows(nc - 1 - s), :]
            for h in range(H):
                advance(2 * h, ckef_ref[rf, kc(h)], cv_ref[rf, vc(h)], df8[:, kc(h)])
                advance(2 * h + 1, ckeb_ref[rb, kc(h)], cv_ref[rb, vc(h)], db8[:, kc(h)])
            return carry

        lax.fori_loop(0, nc, ctx_step, 0, unroll=2)

    @pl.when(i > 0)
    def _():
        def lat_step(s, carry):
            rf, rb = rows(s), rows(n - 1 - s)
            df8, db8 = df_ref[dec_rows(s), :], db_ref[dec_rows(n - 1 - s), :]
            for h in range(H):
                of_ref[rf, vc(h)] = attend(2 * h, qdf_ref[rf, kc(h)])
                advance(2 * h, kef_ref[rf, kc(h)], vf_ref[rf, vc(h)], df8[:, kc(h)])
                ob_ref[rb, vc(h)] = attend(2 * h + 1, qdb_ref[rb, kc(h)])
                advance(2 * h + 1, keb_ref[rb, kc(h)], vb_ref[rb, vc(h)], db8[:, kc(h)])
            return carry

        lax.fori_loop(0, n, lat_step, 0, unroll=4)


def _gla(ctx_t, lat_t, *, B, Tc, T, tl):
    ckef, ckeb, cv, cdf, cdb = ctx_t
    qdf, kef, qdb, keb, v, df, db = lat_t
    nt = T // tl

    def cblk(rows_, width):
        return pl.BlockSpec((rows_, width), lambda b, i: (b, 0))

    def fblk(rows_, width):
        return pl.BlockSpec((rows_, width), lambda b, i: (b * nt + jnp.maximum(i - 1, 0), 0))

    def bblk(rows_, width):
        return pl.BlockSpec((rows_, width), lambda b, i: (b * nt + nt - 1 - jnp.maximum(i - 1, 0), 0))

    o_shape = jax.ShapeDtypeStruct((B * T, GLA_VAL), BF16)
    return pl.pallas_call(
        _gla_kernel,
        out_shape=[o_shape, o_shape],
        grid=(B, 1 + nt),
        in_specs=[cblk(Tc, GLA_KEY), cblk(Tc, GLA_KEY), cblk(Tc, GLA_VAL),
                  cblk(Tc // SUBLANE, GLA_KEY), cblk(Tc // SUBLANE, GLA_KEY),
                  fblk(tl, GLA_KEY), fblk(tl, GLA_KEY), fblk(tl, GLA_VAL), fblk(tl // SUBLANE, GLA_KEY),
                  bblk(tl, GLA_KEY), bblk(tl, GLA_KEY), bblk(tl, GLA_VAL), bblk(tl // SUBLANE, GLA_KEY)],
        out_specs=[fblk(tl, GLA_VAL), bblk(tl, GLA_VAL)],
        scratch_shapes=[pltpu.VMEM((2 * GLA_HEADS, GLA_DV, GLA_DK), F32)],
        compiler_params=_cparams("parallel", "arbitrary"),
        name="gla",
    )(ckef, ckeb, cv, cdf, cdb, qdf, kef, v, df, qdb, keb, v, db)


def _mix_out_kernel(of_ref, ob_ref, qdf_ref, kif_ref, qdb_ref, kib_ref, v_ref,
                    sg_ref, gnw_ref, hf_ref, hb_ref, gy_ref, sa_ref, sb_ref,
                    wgla_ref, wlru_ref, wo_ref, o_ref, oi_ref):
    tm, W = gy_ref.shape
    hsum = (hf_ref[...] + hb_ref[...]).reshape(tm, W)
    hl = (hsum * gy_ref[...].astype(F32)).astype(BF16)
    y_lru = _dot(hl, wlru_ref[...])
    R = _pick_tile(tm, 2 * LANE, GLA_CHUNK)
    ri = lax.broadcasted_iota(jnp.int32, (R, R), 0)
    ci = lax.broadcasted_iota(jnp.int32, (R, R), 1)
    same = (ri // GLA_CHUNK) == (ci // GLA_CHUNK)
    fwd_ok = same & (ci <= ri)
    bwd_ok = same & (ci > ri)
    og = []
    for h in range(GLA_HEADS):
        ks = slice(h * GLA_DK, (h + 1) * GLA_DK)
        cs = slice(h * GLA_DV, (h + 1) * GLA_DV)
        for c in range(tm // R):
            r = slice(c * R, (c + 1) * R)
            sf = _dot_nt(qdf_ref[r, ks], kif_ref[r, ks])
            sb = _dot_nt(qdb_ref[r, ks], kib_ref[r, ks])
            sc = jnp.where(fwd_ok, sf, jnp.where(bwd_ok, sb, 0.0))
            oi_ref[r, cs] = _dot(sc.astype(BF16), v_ref[r, cs])
        o = oi_ref[:, cs] + of_ref[:, cs].astype(F32) + ob_ref[:, cs].astype(F32)
        og.append((_rms(o) * gnw_ref[...] * sg_ref[:, cs].astype(F32)).astype(BF16))
    y_gla = _dot(jnp.concatenate(og, axis=1), wgla_ref[...])
    merged = sa_ref[...].astype(F32) * y_gla + sb_ref[...].astype(F32) * y_lru
    o_ref[...] = _dot(merged.astype(BF16), wo_ref[...]).astype(o_ref.dtype)


def _mix_out(of, ob, qdf, kif, qdb, kib, v, sg, gnw, hf, hb, gy, sa, sb, wgla, wlru, wo, *, B, T, tm):
    rows = of.shape[0]
    W, D = wlru.shape
    nt = T // tm
    tok = lambda width: pl.BlockSpec((tm, width), lambda b, j: (b * nt + j, 0))
    lru = pl.BlockSpec((tm // SUBLANE, None, SUBLANE, W), lambda b, j: (j, b, 0, 0))
    return pl.pallas_call(
        _mix_out_kernel,
        out_shape=jax.ShapeDtypeStruct((rows, D), BF16),
        grid=(B, nt),
        in_specs=[tok(GLA_VAL), tok(GLA_VAL),
                  tok(GLA_KEY), tok(GLA_KEY), tok(GLA_KEY), tok(GLA_KEY), tok(GLA_VAL),
                  tok(GLA_VAL), _const_spec((1, GLA_DV)),
                  lru, lru, tok(W), tok(D), tok(D),
                  _const_spec(wgla.shape), _const_spec(wlru.shape), _const_spec(wo.shape)],
        out_specs=tok(D),
        scratch_shapes=[pltpu.VMEM((tm, GLA_VAL), F32)],
        compiler_params=_cparams("parallel", "parallel"),
        name="mix_out",
    )(of, ob, qdf, kif, qdb, kib, v, sg, gnw.reshape(1, GLA_DV), hf, hb, gy, sa, sb, wgla, wlru, wo)


def _prep_mixer_weights(w_in, fup, fb, conv_w, conv_b, wr, br, wi, bi, lam):
    D = w_in.shape[0]
    o = 0
    wqkv = w_in[:, o:o + 2 * GLA_KEY + GLA_VAL]
    o += 2 * GLA_KEY + GLA_VAL
    w_g = w_in[:, o:o + GLA_VAL]
    o += GLA_VAL
    wfd = w_in[:, o:o + 2 * GLA_RANK]
    o += 2 * GLA_RANK
    W = lam.shape[1]
    wxl = w_in[:, o:o + W]
    o += W
    w_tail = w_in[:, o:]
    wfd_p = jnp.zeros((D, LANE), F32).at[:, :2 * GLA_RANK].set(wfd)
    fup_p = jnp.zeros((LANE, 2 * GLA_KEY), F32)
    fup_p = fup_p.at[:GLA_RANK, :GLA_KEY].set(fup[0]).at[GLA_RANK:2 * GLA_RANK, GLA_KEY:].set(fup[1])
    wg = 0.5 * jnp.concatenate([wr[0], wi[0], wr[1], wi[1]], axis=-1)
    bw = W // LRU_BLOCKS
    bg = 0.5 * jnp.stack([br[0], bi[0], br[1], bi[1]])
    bg = bg.reshape(4, LRU_BLOCKS, bw).transpose(1, 0, 2).reshape(LRU_BLOCKS, 1, 4 * bw)
    bg_hi = bg.astype(BF16)
    bg_lo = (bg - bg_hi.astype(F32)).astype(BF16)
    wg_aug = jnp.concatenate([wg.astype(BF16), bg_hi, bg_lo,
                              jnp.zeros((LRU_BLOCKS, bw - 2, 4 * bw), BF16)], axis=1)
    return {
        "wqkv": wqkv.astype(BF16),
        "wfd": wfd_p.astype(BF16),
        "fup": fup_p.astype(BF16),
        "fb": jnp.concatenate([fb[0], fb[1]]).reshape(1, 2 * GLA_KEY),
        "wxl": wxl.astype(BF16),
        "cw": conv_w,
        "cb": conv_b.reshape(1, W),
        "wg": wg_aug,
        "lam": lam,
        "wrest": jnp.concatenate([w_g, w_tail], axis=1).astype(BF16),
    }


def kernel(x, c, ctx, c_ctx, w_ada, b_ada, norm_w, ffn1_wi, ffn1_wo, ffn2_wi, ffn2_wo, w_in, gla_fup, gla_fb, gla_norm_w, conv_w, conv_b, lru_wr, lru_br, lru_wi, lru_bi, lru_lam, w_out_gla, w_out_lru, w_o, final_norm_w):
    B, T, D = x.shape
    Tc = ctx.shape[1]
    depth = w_ada.shape[0]
    assert depth == 1, "single trunk layer"
    assert B % SUBLANE == 0 and T % GRID_W == 0 and Tc % GLA_CHUNK == 0
    l = 0

    R = -(-(B + 1) // SUBLANE) * SUBLANE
    cs = jnp.zeros((R, D), F32).at[:B].set(c).at[B].set(c_ctx)
    mods = _ada(cs, w_ada, b_ada, l).reshape(R, N_MOD, D)

    x2 = x.reshape(B * T, D)
    c2 = ctx.reshape(B * Tc, D)
    tm_lat = _pick_tile(T, 512, GRID_W)
    wi1, wo1 = ffn1_wi[l].astype(BF16), ffn1_wo[l].astype(BF16)
    wi2, wo2 = ffn2_wi[l].astype(BF16), ffn2_wo[l].astype(BF16)
    tm_ffn = _pick_tile(T, 1024, GRID_W)
    nt_ffn = T // tm_ffn
    tm_cf = _pick_tile(B * Tc, 1024, SUBLANE)

    h = _ffn(x2, mods, lambda i: i // nt_ffn, norm_w[l, 0], wi1, wo1, final_norm_w,
             sub=0, tm=tm_ffn, final_norm=False)
    hc = _ffn(c2, mods, lambda i: B, norm_w[l, 0], wi1, wo1, final_norm_w,
              sub=0, tm=tm_cf, final_norm=False)

    wts = _prep_mixer_weights(w_in[l], gla_fup[l], gla_fb[l], conv_w[l], conv_b[l],
                              lru_wr[l], lru_br[l], lru_wi[l], lru_bi[l], lru_lam[l])
    W = lru_lam.shape[-1]
    ctx_o = _mix_in(hc, mods, lambda b: B, norm_w[l, 1], wts,
                    B=B, L=Tc, tm=Tc, seg=Tc, need_out=False)
    ckef, ckeb, cv, cdf, cdb, cxc = ctx_o
    lat_o = _mix_in(h, mods, lambda b: b, norm_w[l, 1], wts,
                    B=B, L=T, tm=tm_lat, seg=GRID_W, need_out=True)
    (qdf, kif, kef, qdb, kib, keb, v, df, db, xc, sg, gy, sa, sb) = lat_o

    zero = jnp.zeros((B, W), F32)
    hf_c, hb_c = _lru(cxc.reshape(Tc * B, W), wts["wg"], wts["lam"], zero, zero,
                      B=B, need_out=False)
    hf, hb, _, _ = _lru(xc.reshape(T * B, W), wts["wg"], wts["lam"], hf_c, hb_c,
                        B=B, need_out=True)

    of, ob = _gla((ckef, ckeb, cv, cdf, cdb), (qdf, kef, qdb, keb, v, df, db),
                  B=B, Tc=Tc, T=T, tl=_pick_tile(T, 1024, GLA_CHUNK))

    oct_ = lambda t: t.reshape(T // SUBLANE, B, SUBLANE, W)
    y = _mix_out(of, ob, qdf, kif, qdb, kib, v, sg, gla_norm_w[l], oct_(hf), oct_(hb), gy, sa, sb,
                 w_out_gla[l].astype(BF16), w_out_lru[l].astype(BF16), w_o[l].astype(BF16),
                 B=B, T=T, tm=tm_lat)

    out = _ffn(h, mods, lambda i: i // nt_ffn, norm_w[l, 2], wi2, wo2, final_norm_w,
               sub=2, tm=tm_ffn, final_norm=True, branch=y, branch_sub=1)
    return out.reshape(B, T, D)
```

```python
import functools

import jax
import jax.numpy as jnp
from jax import lax
from jax.experimental import pallas as pl
from jax.experimental.pallas import tpu as pltpu

F32 = jnp.float32
BF16 = jnp.bfloat16

N_MOD = 9
GLA_HEADS = 4
GLA_DK = 128
GLA_DV = 256
GLA_KEY = GLA_HEADS * GLA_DK
GLA_VAL = GLA_HEADS * GLA_DV
GLA_RANK = 16
GLA_TAU = 16.0
GLA_CHUNK = 64
GRID_W = 64
LRU_BLOCKS = 8
LRU_C = 8.0
CONV_K = 4
EPS = 1e-6

LANE = 128
SUBLANE = 8
VMEM_LIMIT = 56 * 1024 * 1024


def _cparams(*sem):
    return pltpu.CompilerParams(dimension_semantics=sem, vmem_limit_bytes=VMEM_LIMIT)


def _const_spec(shape):
    nd = len(shape)
    return pl.BlockSpec(shape, lambda *_: (0,) * nd, pipeline_mode=pl.Buffered(1))


def _dot(a, b):
    return jnp.dot(a, b, preferred_element_type=F32)


def _dot_tn(a, b):
    return lax.dot_general(a, b, (((0,), (0,)), ((), ())), preferred_element_type=F32)


def _dot_nt(a, b):
    return lax.dot_general(a, b, (((1,), (1,)), ((), ())), preferred_element_type=F32)


LOG2E = 1.4426950408889634
F32_TINY = 1.1754943508222875e-38


def _sigmoid(x):
    return 1.0 / (1.0 + jnp.exp2(x * (-LOG2E)))


def _sqrt_nonneg(x):
    return x * lax.rsqrt(jnp.maximum(x, F32_TINY))


def _softplus(x):
    return jnp.maximum(x, 0.0) + jnp.log1p(jnp.exp(-jnp.abs(x)))


def _gelu_tanh(x):
    c = 0.7978845608028654
    hx = 0.5 * x
    return hx + hx * jnp.tanh(x * (c + (c * 0.044715) * (x * x)))


def _rms(x):
    return x * lax.rsqrt(jnp.mean(x * x, axis=-1, keepdims=True) + EPS)


def _pick_tile(n, target, quantum):
    best = None
    t = quantum
    while t <= min(n, target):
        if n % t == 0:
            best = t
        t += quantum
    assert best is not None, (n, target, quantum)
    return best


def _ada_kernel(c_ref, w_ref, b_ref, o_ref):
    c = c_ref[...]
    s = (c * _sigmoid(c)).astype(BF16)
    o_ref[...] = _dot(s, w_ref[...].astype(BF16)) + b_ref[...]


def _ada(cs, w_ada, b_ada, layer):
    R, D = cs.shape
    N = w_ada.shape[2]
    tn = _pick_tile(N, 2304, LANE)
    return pl.pallas_call(
        _ada_kernel,
        out_shape=jax.ShapeDtypeStruct((R, N), F32),
        grid=(N // tn,),
        in_specs=[pl.BlockSpec((R, D), lambda j: (0, 0)),
                  pl.BlockSpec((None, D, tn), lambda j: (layer, 0, j)),
                  pl.BlockSpec((1, tn), lambda j: (0, j))],
        out_specs=pl.BlockSpec((R, tn), lambda j: (0, j)),
        compiler_params=_cparams("parallel"),
        name="ada",
    )(cs, w_ada, b_ada[layer].reshape(1, N))


def _ffn_kernel(x_ref, *refs, sub, tf, final_norm, branch_sub):
    if branch_sub is None:
        mod_ref, nw_ref, wi_ref, wo_ref, fnw_ref, o_ref, a_ref = refs
        x = x_ref[...]
    else:
        y_ref, mod_ref, nw_ref, wi_ref, wo_ref, fnw_ref, o_ref, a_ref = refs
        x = x_ref[...] + mod_ref[3 * branch_sub + 2:3 * branch_sub + 3, :] * y_ref[...].astype(F32)
    F = wo_ref.shape[0]
    shift = mod_ref[3 * sub:3 * sub + 1, :]
    scale = mod_ref[3 * sub + 1:3 * sub + 2, :]
    gate = mod_ref[3 * sub + 2:3 * sub + 3, :]
    u = (_rms(x) * nw_ref[...] * (1.0 + scale) + shift).astype(BF16)
    for j in range(F // tf):
        g = _dot(u, wi_ref[:, j * tf:(j + 1) * tf])
        up = _dot(u, wi_ref[:, F + j * tf:F + (j + 1) * tf])
        a_ref[:, j * tf:(j + 1) * tf] = (g * _sigmoid(g) * up).astype(BF16)
    acc = _dot(a_ref[...], wo_ref[...])
    out = x + (0.5 * gate) * acc
    if final_norm:
        out = _rms(out) * fnw_ref[...]
    o_ref[...] = out


def _ffn(x2, mods, mod_index, nw, wi, wo, fnw, *, sub, tm, final_norm, branch=None, branch_sub=None):
    rows, D = x2.shape
    F = wo.shape[0]
    tf = 2 * LANE
    assert F % tf == 0
    assert (branch is None) == (branch_sub is None)
    kern = functools.partial(_ffn_kernel, sub=sub, tf=tf, final_norm=final_norm, branch_sub=branch_sub)
    tok = pl.BlockSpec((tm, D), lambda i: (i, 0))
    streams = [x2] if branch is None else [x2, branch]
    return pl.pallas_call(
        kern,
        out_shape=jax.ShapeDtypeStruct((rows, D), F32),
        scratch_shapes=[pltpu.VMEM((tm, F), BF16)],
        grid=(rows // tm,),
        in_specs=[tok] * len(streams) + [
                  pl.BlockSpec((None, N_MOD, D), lambda i: (mod_index(i), 0, 0)),
                  _const_spec((1, D)),
                  _const_spec((D, 2 * F)),
                  _const_spec((F, D)),
                  _const_spec((1, D))],
        out_specs=tok,
        compiler_params=_cparams("parallel"),
        name="ffn_sub%d" % sub,
    )(*streams, mods, nw.reshape(1, D), wi, wo, fnw.reshape(1, D))


def _mix_in_kernel(x_ref, mod_ref, nw_ref, wqkv_ref, wfd_ref, fup_ref, fb_ref, wxl_ref,
                   cw_ref, cb_ref, wrest_ref, *out_refs, seg, need_out):
    if need_out:
        (qdf_ref, kif_ref, kef_ref, qdb_ref, kib_ref, keb_ref, v_ref, decf_ref, decb_ref,
         xc_ref, sg_ref, gy_ref, sa_ref, sb_ref) = out_refs
    else:
        (kef_ref, keb_ref, v_ref, decf_ref, decb_ref, xc_ref) = out_refs
    tm = x_ref.shape[0]
    n_chunks = tm // GLA_CHUNK
    x = x_ref[...]
    shift = mod_ref[3:4, :]
    scale = mod_ref[4:5, :]
    u = (_rms(x) * nw_ref[...] * (1.0 + scale) + shift).astype(BF16)
    D = x.shape[1]
    W = wxl_ref.shape[1]
    CT = 2 * LANE
    assert tm % seg == 0 and seg % SUBLANE == 0 and cw_ref.shape[1] == seg

    def kc(h):
        return slice(h * GLA_DK, (h + 1) * GLA_DK)

    fd = _dot(u, wfd_ref[...]).astype(BF16)
    logits = _dot(fd, fup_ref[...]) + fb_ref[...]

    def xl_tile(j):
        return _dot(u, wxl_ref[:, j * CT:(j + 1) * CT])

    def qkv_tile(j):
        return _dot(u, wqkv_ref[:, j * CT:(j + 1) * CT])

    def rest_tile(j):
        return _dot(u, wrest_ref[:, j * CT:(j + 1) * CT])

    sub_pos = lax.broadcasted_iota(jnp.int32, (1, SUBLANE, 1), 1)
    vregs_per_chunk = GLA_CHUNK // SUBLANE

    def chunk_log_decay(r, h, d):
        lg = logits[r, d * GLA_KEY + h * GLA_DK:d * GLA_KEY + (h + 1) * GLA_DK]
        z = lg * LOG2E
        la = (jnp.minimum(z, 0.0) - jnp.log2(1.0 + jnp.exp2(-jnp.abs(z)))) * (1.0 / GLA_TAU)
        la = la.reshape(vregs_per_chunk, SUBLANE, GLA_DK)
        cs = la
        s = 1
        while s < SUBLANE:
            cs = cs + jnp.where(sub_pos >= s, pltpu.roll(cs, s, 1), 0.0)
            s *= 2
        last = cs[:, SUBLANE - 1:SUBLANE, :]
        carry = [jnp.zeros((1, 1, GLA_DK), F32)]
        for g in range(1, vregs_per_chunk):
            carry.append(carry[-1] + last[g - 1:g])
        tot = carry[-1] + last[vregs_per_chunk - 1:]
        return la, cs + jnp.concatenate(carry, axis=0), tot

    def conv(j, xl):
        cs = slice(j * CT, (j + 1) * CT)
        xc = cb_ref[:, cs][None]
        for t in range(CONV_K):
            off = t - CONV_K // 2
            term = xl if off == 0 else pltpu.roll(xl, (-off) % tm, 0)
            xc = xc + term.reshape(tm // seg, seg, CT) * cw_ref[t, :, cs][None]
        xc_ref[:, :, cs] = xc.reshape(tm // SUBLANE, SUBLANE, CT)

    tiles = {}

    def decayed_products(h, c):
        r = slice(c * GLA_CHUNK, (c + 1) * GLA_CHUNK)
        cols = slice((h % hpt) * GLA_DK, (h % hpt + 1) * GLA_DK)
        shape3 = (vregs_per_chunk, SUBLANE, GLA_DK)
        _, b_f, tot_f = chunk_log_decay(r, h, 0)
        la_b, cs_b, tot_b = chunk_log_decay(r, h, 1)
        pre_b = cs_b - la_b
        c_b = tot_b - pre_b
        e_f = tot_f - b_f
        decf_ref[c * SUBLANE:(c + 1) * SUBLANE, kc(h)] = jnp.broadcast_to(jnp.exp2(tot_f[0]), (SUBLANE, GLA_DK))
        decb_ref[c * SUBLANE:(c + 1) * SUBLANE, kc(h)] = jnp.broadcast_to(jnp.exp2(tot_b[0]), (SUBLANE, GLA_DK))

        def put(ref, val):
            ref[r, kc(h)] = val.reshape(GLA_CHUNK, GLA_DK).astype(BF16)

        k = tiles["k", h // hpt][r, cols].reshape(shape3)
        put(kef_ref, k * jnp.exp2(e_f))
        put(keb_ref, k * jnp.exp2(pre_b))
        if need_out:
            put(kif_ref, k * jnp.exp2(-b_f))
            put(kib_ref, k * jnp.exp2(-c_b))
            q = tiles["q", h // hpt][r, cols].reshape(shape3) * (GLA_DK ** -0.5)
            put(qdf_ref, q * jnp.exp2(b_f))
            put(qdb_ref, q * jnp.exp2(c_b))

    n_q = GLA_KEY // CT
    n_v = GLA_VAL // CT
    hpt = CT // GLA_DK
    assert GLA_KEY % CT == 0 and GLA_VAL % CT == 0 and W % CT == 0 and D % CT == 0

    mxu, pending = [], []

    def add_mxu(key, fn):
        mxu.append((key, fn))
        return len(mxu) - 1

    for j in range(W // CT):
        step = add_mxu(("xl", j), functools.partial(xl_tile, j))
        pending.append((step, 7, lambda j=j: conv(j, tiles.pop(("xl", j)))))
    last_qk = 0
    for j in range(n_q):
        last_qk = add_mxu(("k", j), functools.partial(qkv_tile, n_q + j))
    if need_out:
        for j in range(n_q):
            last_qk = add_mxu(("q", j), functools.partial(qkv_tile, j))
    for h in range(GLA_HEADS):
        for c in range(n_chunks):
            pending.append((last_qk, 3 if need_out else 2, functools.partial(decayed_products, h, c)))
    for j in range(n_v):
        step = add_mxu(("v", j), functools.partial(qkv_tile, 2 * n_q + j))

        def vcast(j=j):
            v_ref[:, j * CT:(j + 1) * CT] = tiles.pop(("v", j)).astype(BF16)

        pending.append((step, 1, vcast))
    if need_out:
        acts = ((sg_ref, lambda t: t * _sigmoid(t), 3), (gy_ref, _gelu_tanh, 6),
                (sa_ref, _sigmoid, 2), (sb_ref, _sigmoid, 2))
        per = D // CT
        assert GLA_VAL == D and W == D
        for j in range(len(acts) * per):
            step = add_mxu(("rest", j), functools.partial(rest_tile, j))
            o_ref_, fn, cost = acts[j // per]

            def act(j=j, o_ref_=o_ref_, fn=fn):
                o_ref_[:, (j % per) * CT:(j % per + 1) * CT] = fn(tiles.pop(("rest", j))).astype(BF16)

            pending.append((step, cost, act))

    bulk = [p for p in pending if p[2].__class__ is functools.partial]
    quick = [p for p in pending if p[2].__class__ is not functools.partial]
    quick.sort(key=lambda p: p[0])
    per_step = sum(c for _, c, _ in pending) / len(mxu)
    credit = 0.0
    for i, (key, fn) in enumerate(mxu):
        tiles[key] = fn()
        credit += per_step
        for queue in (quick, bulk):
            while queue and queue[0][0] < i and credit > 0:
                step, cost, task = queue.pop(0)
                task()
                credit -= cost
    for step, cost, task in quick + bulk:
        task()


def _segment_tap_weights(conv_w, seg):
    pos = jnp.arange(seg)[None, :, None]
    off = (jnp.arange(CONV_K) - CONV_K // 2)[:, None, None]
    inside = (pos + off >= 0) & (pos + off < seg)
    return jnp.where(inside, conv_w[:, None, :], 0.0)


def _mix_in(x2, mods, mod_index, nw, wts, *, B, L, tm, seg, need_out):
    rows, D = x2.shape
    W = wts["wxl"].shape[1]
    nt = L // tm
    grid = (B, nt)

    def tok(width):
        return pl.BlockSpec((tm, width), lambda b, j: (b * nt + j, 0))

    dec_spec = pl.BlockSpec((tm // SUBLANE, GLA_KEY), lambda b, j: (b * nt + j, 0))
    lru_spec = pl.BlockSpec((tm // SUBLANE, None, SUBLANE, W), lambda b, j: (j, b, 0, 0))
    tokbf = lambda width: jax.ShapeDtypeStruct((rows, width), BF16)
    dec_shape = jax.ShapeDtypeStruct((rows // SUBLANE, GLA_KEY), F32)
    lru_shape = jax.ShapeDtypeStruct((L // SUBLANE, B, SUBLANE, W), F32)

    n_key = 6 if need_out else 2
    out_shape = [tokbf(GLA_KEY)] * n_key + [tokbf(GLA_VAL), dec_shape, dec_shape, lru_shape]
    out_specs = [tok(GLA_KEY)] * n_key + [tok(GLA_VAL), dec_spec, dec_spec, lru_spec]
    if need_out:
        out_shape += [tokbf(GLA_VAL), tokbf(W), tokbf(D), tokbf(D)]
        out_specs += [tok(GLA_VAL), tok(W), tok(D), tok(D)]

    names = ["wqkv", "wfd", "fup", "fb", "wxl", "cw", "cb", "wrest"]
    w_list = [_segment_tap_weights(wts["cw"], seg) if n == "cw" else wts[n] for n in names]
    in_specs = [pl.BlockSpec((tm, D), lambda b, j: (b * nt + j, 0)),
                pl.BlockSpec((None, N_MOD, D), lambda b, j: (mod_index(b), 0, 0)),
                _const_spec((1, D))] + [_const_spec(w.shape) for w in w_list]
    kern = functools.partial(_mix_in_kernel, seg=seg, need_out=need_out)
    return pl.pallas_call(
        kern,
        out_shape=out_shape,
        grid=grid,
        in_specs=in_specs,
        out_specs=out_specs,
        compiler_params=_cparams("parallel", "parallel"),
        name="mix_in_out" if need_out else "mix_in_ctx",
    )(x2, mods, nw.reshape(1, D), *w_list)


def _lru_kernel(xf_ref, xb_ref, wg_ref, lam_ref, hf0_ref, hb0_ref, *refs, B, tt, need_out):
    if need_out:
        hf_ref, hb_ref, hfl_ref, hbl_ref, af_s, uf_s, ab_s, ub_s, sf_ref, sb_ref = refs
    else:
        hfl_ref, hbl_ref, af_s, uf_s, ab_s, ub_s, sf_ref, sb_ref = refs
    bw = xf_ref.shape[1]
    R = B * SUBLANE

    @pl.when(pl.program_id(1) == 0)
    def _():
        sf_ref[...] = hf0_ref[...]
        sb_ref[...] = hb0_ref[...]

    half_log2 = (-0.5 * LRU_C * LOG2E) * _softplus(-lam_ref[...])
    lane = lax.broadcasted_iota(jnp.int32, (R, bw), 1)
    ones = jnp.where(lane < 2, 1.0, 0.0).astype(BF16)

    def gates(x_ref, d, a_s, u_s, rows):
        xc = x_ref[rows, :]
        lhs = jnp.concatenate([xc.astype(BF16), ones], axis=1)
        z = _dot(lhs, wg_ref[:, 2 * d * bw:(2 * d + 2) * bw])
        t_r = jnp.tanh(z[:, :bw])
        t_i = jnp.tanh(z[:, bw:])
        hl2 = half_log2[d:d + 1, :]
        a = jnp.exp2(hl2 + hl2 * t_r)
        a_s[rows, :] = a
        xh = 0.5 * xc
        u_s[rows, :] = _sqrt_nonneg(1.0 - a * a) * (xh + xh * t_i)

    hf = sf_ref[...]
    hb = sb_ref[...]
    n_oct = tt // SUBLANE
    for k in range(n_oct):
        kf, kb = k, n_oct - 1 - k
        gates(xf_ref, 0, af_s, uf_s, slice(kf * R, (kf + 1) * R))
        gates(xb_ref, 1, ab_s, ub_s, slice(kb * R, (kb + 1) * R))
        for t in range(SUBLANE):
            rf = pl.ds(kf * R + t, B, stride=SUBLANE)
            rb = pl.ds(kb * R + SUBLANE - 1 - t, B, stride=SUBLANE)
            hf = af_s[rf, :] * hf + uf_s[rf, :]
            hb = ab_s[rb, :] * hb + ub_s[rb, :]
            if need_out:
                hf_ref[rf, :] = hf
                hb_ref[rb, :] = hb
    sf_ref[...] = hf
    sb_ref[...] = hb
    hfl_ref[...] = hf
    hbl_ref[...] = hb


def _lru(xc, wg, lam, hf0, hb0, *, B, need_out):
    rows, W = xc.shape
    L = rows // B
    tt = SUBLANE * _pick_tile(L // SUBLANE, 32, 1)
    wb = LANE
    assert W // LRU_BLOCKS == wb
    nblk = L // tt
    fwd = pl.BlockSpec((tt * B, wb), lambda w, i: (i, w))
    bwd = pl.BlockSpec((tt * B, wb), lambda w, i: (nblk - 1 - i, w))
    st = pl.BlockSpec((B, wb), lambda w, i: (0, w))
    out_shape = [jax.ShapeDtypeStruct((B, W), F32)] * 2
    out_specs = [st, st]
    if need_out:
        out_shape = [jax.ShapeDtypeStruct((rows, W), F32)] * 2 + out_shape
        out_specs = [fwd, bwd] + out_specs
    kern = functools.partial(_lru_kernel, B=B, tt=tt, need_out=need_out)
    return pl.pallas_call(
        kern,
        out_shape=out_shape,
        grid=(W // wb, nblk),
        in_specs=[fwd, bwd,
                  pl.BlockSpec((None, 2 * wb, 4 * wb), lambda w, i: (w, 0, 0)),
                  pl.BlockSpec((2, wb), lambda w, i: (0, w)),
                  st, st],
        out_specs=out_specs,
        scratch_shapes=[pltpu.VMEM((tt * B, wb), F32)] * 4 + [pltpu.VMEM((B, wb), F32)] * 2,
        compiler_params=_cparams("parallel", "arbitrary"),
        name="lru_out" if need_out else "lru_ctx",
    )(xc, xc, wg, lam, hf0, hb0)


def _gla_kernel(ckef_ref, ckeb_ref, cv_ref, cdf_ref, cdb_ref,
                qdf_ref, kef_ref, vf_ref, df_ref,
                qdb_ref, keb_ref, vb_ref, db_ref,
                of_ref, ob_ref, st_ref):
    C = GLA_CHUNK
    H = GLA_HEADS
    nc = cv_ref.shape[0] // C
    n = vf_ref.shape[0] // C

    def rows(c):
        return pl.ds(pl.multiple_of(c * C, C), C)

    def dec_rows(c):
        return pl.ds(pl.multiple_of(c * SUBLANE, SUBLANE), SUBLANE)

    def kc(h):
        return slice(h * GLA_DK, (h + 1) * GLA_DK)

    def vc(h):
        return slice(h * GLA_DV, (h + 1) * GLA_DV)

    def advance(idx, ke, v, dec8):
        st_ref[idx] = st_ref[idx] * dec8[0:1, :] + _dot_tn(v, ke)

    def attend(idx, qd):
        return _dot_nt(qd, st_ref[idx].astype(BF16)).astype(BF16)

    i = pl.program_id(1)

    @pl.when(i == 0)
    def _():
        st_ref[...] = jnp.zeros(st_ref.shape, F32)

        def ctx_step(s, carry):
            rf, rb = rows(s), rows(nc - 1 - s)
            df8, db8 = cdf_ref[dec_rows(s), :], cdb_ref[dec_rows(nc - 1 - s), :]
            for h in range(H):
                advance(2 * h, ckef_ref[rf, kc(h)], cv_ref[rf, vc(h)], df8[:, kc(h)])
                advance(2 * h + 1, ckeb_ref[rb, kc(h)], cv_ref[rb, vc(h)], db8[:, kc(h)])
            return carry

        lax.fori_loop(0, nc, ctx_step, 0, unroll=4)

    @pl.when(i > 0)
    def _():
        def lat_step(s, carry):
            rf, rb = rows(s), rows(n - 1 - s)
            df8, db8 = df_ref[dec_rows(s), :], db_ref[dec_rows(n - 1 - s), :]
            for h in range(H):
                of_ref[rf, vc(h)] = attend(2 * h, qdf_ref[rf, kc(h)])
                advance(2 * h, kef_ref[rf, kc(h)], vf_ref[rf, vc(h)], df8[:, kc(h)])
                ob_ref[rb, vc(h)] = attend(2 * h + 1, qdb_ref[rb, kc(h)])
                advance(2 * h + 1, keb_ref[rb, kc(h)], vb_ref[rb, vc(h)], db8[:, kc(h)])
            return carry

        lax.fori_loop(0, n, lat_step, 0, unroll=8)


def _gla(ctx_t, lat_t, *, B, Tc, T, tl):
    ckef, ckeb, cv, cdf, cdb = ctx_t
    qdf, kef, qdb, keb, v, df, db = lat_t
    nt = T // tl

    def cblk(rows_, width):
        return pl.BlockSpec((rows_, width), lambda b, i: (b, 0))

    def fblk(rows_, width):
        return pl.BlockSpec((rows_, width), lambda b, i: (b * nt + jnp.maximum(i - 1, 0), 0))

    def bblk(rows_, width):
        return pl.BlockSpec((rows_, width), lambda b, i: (b * nt + nt - 1 - jnp.maximum(i - 1, 0), 0))

    o_shape = jax.ShapeDtypeStruct((B * T, GLA_VAL), BF16)
    return pl.pallas_call(
        _gla_kernel,
        out_shape=[o_shape, o_shape],
        grid=(B, 1 + nt),
        in_specs=[cblk(Tc, GLA_KEY), cblk(Tc, GLA_KEY), cblk(Tc, GLA_VAL),
                  cblk(Tc // SUBLANE, GLA_KEY), cblk(Tc // SUBLANE, GLA_KEY),
                  fblk(tl, GLA_KEY), fblk(tl, GLA_KEY), fblk(tl, GLA_VAL), fblk(tl // SUBLANE, GLA_KEY),
                  bblk(tl, GLA_KEY), bblk(tl, GLA_KEY), bblk(tl, GLA_VAL), bblk(tl // SUBLANE, GLA_KEY)],
        out_specs=[fblk(tl, GLA_VAL), bblk(tl, GLA_VAL)],
        scratch_shapes=[pltpu.VMEM((2 * GLA_HEADS, GLA_DV, GLA_DK), F32)],
        compiler_params=_cparams("parallel", "arbitrary"),
        name="gla",
    )(ckef, ckeb, cv, cdf, cdb, qdf, kef, v, df, qdb, keb, v, db)


def _mix_out_kernel(of_ref, ob_ref, qdf_ref, kif_ref, qdb_ref, kib_ref, v_ref,
                    sg_ref, gnw_ref, hf_ref, hb_ref, gy_ref, sa_ref, sb_ref,
                    wgla_ref, wlru_ref, wo_ref, o_ref, oi_ref):
    tm, W = gy_ref.shape
    hsum = (hf_ref[...] + hb_ref[...]).reshape(tm, W)
    hl = (hsum * gy_ref[...].astype(F32)).astype(BF16)
    y_lru = _dot(hl, wlru_ref[...])
    R = _pick_tile(tm, 2 * LANE, GLA_CHUNK)
    ri = lax.broadcasted_iota(jnp.int32, (R, R), 0)
    ci = lax.broadcasted_iota(jnp.int32, (R, R), 1)
    same = (ri // GLA_CHUNK) == (ci // GLA_CHUNK)
    fwd_ok = same & (ci <= ri)
    bwd_ok = same & (ci > ri)
    og = []
    for h in range(GLA_HEADS):
        ks = slice(h * GLA_DK, (h + 1) * GLA_DK)
        cs = slice(h * GLA_DV, (h + 1) * GLA_DV)
        for c in range(tm // R):
            r = slice(c * R, (c + 1) * R)
            sf = _dot_nt(qdf_ref[r, ks], kif_ref[r, ks])
            sb = _dot_nt(qdb_ref[r, ks], kib_ref[r, ks])
            sc = jnp.where(fwd_ok, sf, jnp.where(bwd_ok, sb, 0.0))
            oi_ref[r, cs] = _dot(sc.astype(BF16), v_ref[r, cs])
        o = oi_ref[:, cs] + of_ref[:, cs].astype(F32) + ob_ref[:, cs].astype(F32)
        og.append((_rms(o) * gnw_ref[...] * sg_ref[:, cs].astype(F32)).astype(BF16))
    y_gla = _dot(jnp.concatenate(og, axis=1), wgla_ref[...])
    merged = sa_ref[...].astype(F32) * y_gla + sb_ref[...].astype(F32) * y_lru
    o_ref[...] = _dot(merged.astype(BF16), wo_ref[...]).astype(o_ref.dtype)


def _mix_out(of, ob, qdf, kif, qdb, kib, v, sg, gnw, hf, hb, gy, sa, sb, wgla, wlru, wo, *, B, T, tm):
    rows = of.shape[0]
    W, D = wlru.shape
    nt = T // tm
    tok = lambda width: pl.BlockSpec((tm, width), lambda b, j: (b * nt + j, 0))
    lru = pl.BlockSpec((tm // SUBLANE, None, SUBLANE, W), lambda b, j: (j, b, 0, 0))
    return pl.pallas_call(
        _mix_out_kernel,
        out_shape=jax.ShapeDtypeStruct((rows, D), BF16),
        grid=(B, nt),
        in_specs=[tok(GLA_VAL), tok(GLA_VAL),
                  tok(GLA_KEY), tok(GLA_KEY), tok(GLA_KEY), tok(GLA_KEY), tok(GLA_VAL),
                  tok(GLA_VAL), _const_spec((1, GLA_DV)),
                  lru, lru, tok(W), tok(D), tok(D),
                  _const_spec(wgla.shape), _const_spec(wlru.shape), _const_spec(wo.shape)],
        out_specs=tok(D),
        scratch_shapes=[pltpu.VMEM((tm, GLA_VAL), F32)],
        compiler_params=_cparams("parallel", "parallel"),
        name="mix_out",
    )(of, ob, qdf, kif, qdb, kib, v, sg, gnw.reshape(1, GLA_DV), hf, hb, gy, sa, sb, wgla, wlru, wo)


def _prep_mixer_weights(w_in, fup, fb, conv_w, conv_b, wr, br, wi, bi, lam):
    D = w_in.shape[0]
    o = 0
    wqkv = w_in[:, o:o + 2 * GLA_KEY + GLA_VAL]
    o += 2 * GLA_KEY + GLA_VAL
    w_g = w_in[:, o:o + GLA_VAL]
    o += GLA_VAL
    wfd = w_in[:, o:o + 2 * GLA_RANK]
    o += 2 * GLA_RANK
    W = lam.shape[1]
    wxl = w_in[:, o:o + W]
    o += W
    w_tail = w_in[:, o:]
    wfd_p = jnp.zeros((D, LANE), F32).at[:, :2 * GLA_RANK].set(wfd)
    fup_p = jnp.zeros((LANE, 2 * GLA_KEY), F32)
    fup_p = fup_p.at[:GLA_RANK, :GLA_KEY].set(fup[0]).at[GLA_RANK:2 * GLA_RANK, GLA_KEY:].set(fup[1])
    wg = 0.5 * jnp.concatenate([wr[0], wi[0], wr[1], wi[1]], axis=-1)
    bw = W // LRU_BLOCKS
    bg = 0.5 * jnp.stack([br[0], bi[0], br[1], bi[1]])
    bg = bg.reshape(4, LRU_BLOCKS, bw).transpose(1, 0, 2).reshape(LRU_BLOCKS, 1, 4 * bw)
    bg_hi = bg.astype(BF16)
    bg_lo = (bg - bg_hi.astype(F32)).astype(BF16)
    wg_aug = jnp.concatenate([wg.astype(BF16), bg_hi, bg_lo,
                              jnp.zeros((LRU_BLOCKS, bw - 2, 4 * bw), BF16)], axis=1)
    return {
        "wqkv": wqkv.astype(BF16),
        "wfd": wfd_p.astype(BF16),
        "fup": fup_p.astype(BF16),
        "fb": jnp.concatenate([fb[0], fb[1]]).reshape(1, 2 * GLA_KEY),
        "wxl": wxl.astype(BF16),
        "cw": conv_w,
        "cb": conv_b.reshape(1, W),
        "wg": wg_aug,
        "lam": lam,
        "wrest": jnp.concatenate([w_g, w_tail], axis=1).astype(BF16),
    }


def kernel(x, c, ctx, c_ctx, w_ada, b_ada, norm_w, ffn1_wi, ffn1_wo, ffn2_wi, ffn2_wo, w_in, gla_fup, gla_fb, gla_norm_w, conv_w, conv_b, lru_wr, lru_br, lru_wi, lru_bi, lru_lam, w_out_gla, w_out_lru, w_o, final_norm_w):
    B, T, D = x.shape
    Tc = ctx.shape[1]
    depth = w_ada.shape[0]
    assert depth == 1, "single trunk layer"
    assert B % SUBLANE == 0 and T % GRID_W == 0 and Tc % GLA_CHUNK == 0
    l = 0

    R = -(-(B + 1) // SUBLANE) * SUBLANE
    cs = jnp.zeros((R, D), F32).at[:B].set(c).at[B].set(c_ctx)
    mods = _ada(cs, w_ada, b_ada, l).reshape(R, N_MOD, D)

    x2 = x.reshape(B * T, D)
    c2 = ctx.reshape(B * Tc, D)
    tm_lat = _pick_tile(T, 512, GRID_W)
    wi1, wo1 = ffn1_wi[l].astype(BF16), ffn1_wo[l].astype(BF16)
    wi2, wo2 = ffn2_wi[l].astype(BF16), ffn2_wo[l].astype(BF16)
    tm_ffn = _pick_tile(T, 1024, GRID_W)
    nt_ffn = T // tm_ffn
    tm_cf = _pick_tile(B * Tc, 1024, SUBLANE)

    h = _ffn(x2, mods, lambda i: i // nt_ffn, norm_w[l, 0], wi1, wo1, final_norm_w,
             sub=0, tm=tm_ffn, final_norm=False)
    hc = _ffn(c2, mods, lambda i: B, norm_w[l, 0], wi1, wo1, final_norm_w,
              sub=0, tm=tm_cf, final_norm=False)

    wts = _prep_mixer_weights(w_in[l], gla_fup[l], gla_fb[l], conv_w[l], conv_b[l],
                              lru_wr[l], lru_br[l], lru_wi[l], lru_bi[l], lru_lam[l])
    W = lru_lam.shape[-1]
    ctx_o = _mix_in(hc, mods, lambda b: B, norm_w[l, 1], wts,
                    B=B, L=Tc, tm=Tc, seg=Tc, need_out=False)
    ckef, ckeb, cv, cdf, cdb, cxc = ctx_o
    lat_o = _mix_in(h, mods, lambda b: b, norm_w[l, 1], wts,
                    B=B, L=T, tm=tm_lat, seg=GRID_W, need_out=True)
    (qdf, kif, kef, qdb, kib, keb, v, df, db, xc, sg, gy, sa, sb) = lat_o

    zero = jnp.zeros((B, W), F32)
    hf_c, hb_c = _lru(cxc.reshape(Tc * B, W), wts["wg"], wts["lam"], zero, zero,
                      B=B, need_out=False)
    hf, hb, _, _ = _lru(xc.reshape(T * B, W), wts["wg"], wts["lam"], hf_c, hb_c,
                        B=B, need_out=True)

    of, ob = _gla((ckef, ckeb, cv, cdf, cdb), (qdf, kef, qdb, keb, v, df, db),
                  B=B, Tc=Tc, T=T, tl=_pick_tile(T, 1024, GLA_CHUNK))

    oct_ = lambda t: t.reshape(T // SUBLANE, B, SUBLANE, W)
    y = _mix_out(of, ob, qdf, kif, qdb, kib, v, sg, gla_norm_w[l], oct_(hf), oct_(hb), gy, sa, sb,
                 w_out_gla[l].astype(BF16), w_out_lru[l].astype(BF16), w_o[l].astype(BF16),
                 B=B, T=T, tm=tm_lat)

    out = _ffn(h, mods, lambda i: i // nt_ffn, norm_w[l, 2], wi2, wo2, final_norm_w,
               sub=2, tm=tm_ffn, final_norm=True, branch=y, branch_sub=1)
    return out.reshape(B, T, D)
```

```python
import functools

import jax
import jax.numpy as jnp
from jax import lax
from jax.experimental import pallas as pl
from jax.experimental.pallas import tpu as pltpu

F32 = jnp.float32
BF16 = jnp.bfloat16

N_MOD = 9
GLA_HEADS = 4
GLA_DK = 128
GLA_DV = 256
GLA_KEY = GLA_HEADS * GLA_DK
GLA_VAL = GLA_HEADS * GLA_DV
GLA_RANK = 16
GLA_TAU = 16.0
GLA_CHUNK = 64
GRID_W = 64
LRU_BLOCKS = 8
LRU_C = 8.0
CONV_K = 4
EPS = 1e-6

LANE = 128
SUBLANE = 8
VMEM_LIMIT = 56 * 1024 * 1024


def _cparams(*sem):
    return pltpu.CompilerParams(dimension_semantics=sem, vmem_limit_bytes=VMEM_LIMIT)


def _const_spec(shape):
    nd = len(shape)
    return pl.BlockSpec(shape, lambda *_: (0,) * nd, pipeline_mode=pl.Buffered(1))


def _dot(a, b):
    return jnp.dot(a, b, preferred_element_type=F32)


def _dot_tn(a, b):
    return lax.dot_general(a, b, (((0,), (0,)), ((), ())), preferred_element_type=F32)


def _dot_nt(a, b):
    return lax.dot_general(a, b, (((1,), (1,)), ((), ())), preferred_element_type=F32)


LOG2E = 1.4426950408889634
F32_TINY = 1.1754943508222875e-38


def _sigmoid(x):
    return 1.0 / (1.0 + jnp.exp2(x * (-LOG2E)))


def _sqrt_nonneg(x):
    return x * lax.rsqrt(jnp.maximum(x, F32_TINY))


def _softplus(x):
    return jnp.maximum(x, 0.0) + jnp.log1p(jnp.exp(-jnp.abs(x)))


def _gelu_tanh(x):
    c = 0.7978845608028654
    hx = 0.5 * x
    return hx + hx * jnp.tanh(x * (c + (c * 0.044715) * (x * x)))


def _rms(x):
    return x * lax.rsqrt(jnp.mean(x * x, axis=-1, keepdims=True) + EPS)


def _pick_tile(n, target, quantum):
    best = None
    t = quantum
    while t <= min(n, target):
        if n % t == 0:
            best = t
        t += quantum
    assert best is not None, (n, target, quantum)
    return best


def _ada_kernel(c_ref, w_ref, b_ref, o_ref):
    c = c_ref[...]
    s = (c * _sigmoid(c)).astype(BF16)
    o_ref[...] = _dot(s, w_ref[...].astype(BF16)) + b_ref[...]


def _ada(cs, w_ada, b_ada, layer):
    R, D = cs.shape
    N = w_ada.shape[2]
    tn = _pick_tile(N, 2304, LANE)
    return pl.pallas_call(
        _ada_kernel,
        out_shape=jax.ShapeDtypeStruct((R, N), F32),
        grid=(N // tn,),
        in_specs=[pl.BlockSpec((R, D), lambda j: (0, 0)),
                  pl.BlockSpec((None, D, tn), lambda j: (layer, 0, j)),
                  pl.BlockSpec((1, tn), lambda j: (0, j))],
        out_specs=pl.BlockSpec((R, tn), lambda j: (0, j)),
        compiler_params=_cparams("parallel"),
        name="ada",
    )(cs, w_ada, b_ada[layer].reshape(1, N))


def _ffn_kernel(x_ref, *refs, sub, tf, final_norm, branch_sub):
    if branch_sub is None:
        mod_ref, nw_ref, wi_ref, wo_ref, fnw_ref, o_ref, a_ref = refs
        x = x_ref[...]
    else:
        y_ref, mod_ref, nw_ref, wi_ref, wo_ref, fnw_ref, o_ref, a_ref = refs
        x = x_ref[...] + mod_ref[3 * branch_sub + 2:3 * branch_sub + 3, :] * y_ref[...].astype(F32)
    F = wo_ref.shape[0]
    shift = mod_ref[3 * sub:3 * sub + 1, :]
    scale = mod_ref[3 * sub + 1:3 * sub + 2, :]
    gate = mod_ref[3 * sub + 2:3 * sub + 3, :]
    u = (_rms(x) * (nw_ref[...] * (1.0 + scale)) + shift).astype(BF16)
    for j in range(F // tf):
        g = _dot(u, wi_ref[:, j * tf:(j + 1) * tf])
        up = _dot(u, wi_ref[:, F + j * tf:F + (j + 1) * tf])
        a_ref[:, j * tf:(j + 1) * tf] = (g * _sigmoid(g) * up).astype(BF16)
    acc = _dot(a_ref[...], wo_ref[...])
    out = x + (0.5 * gate) * acc
    if final_norm:
        out = _rms(out) * fnw_ref[...]
    o_ref[...] = out


def _ffn(x2, mods, mod_index, nw, wi, wo, fnw, *, sub, tm, final_norm, branch=None, branch_sub=None):
    rows, D = x2.shape
    F = wo.shape[0]
    tf = 2 * LANE
    assert F % tf == 0
    assert (branch is None) == (branch_sub is None)
    kern = functools.partial(_ffn_kernel, sub=sub, tf=tf, final_norm=final_norm, branch_sub=branch_sub)
    tok = pl.BlockSpec((tm, D), lambda i: (i, 0))
    streams = [x2] if branch is None else [x2, branch]
    return pl.pallas_call(
        kern,
        out_shape=jax.ShapeDtypeStruct((rows, D), F32),
        scratch_shapes=[pltpu.VMEM((tm, F), BF16)],
        grid=(rows // tm,),
        in_specs=[tok] * len(streams) + [
                  pl.BlockSpec((None, N_MOD, D), lambda i: (mod_index(i), 0, 0)),
                  _const_spec((1, D)),
                  _const_spec((D, 2 * F)),
                  _const_spec((F, D)),
                  _const_spec((1, D))],
        out_specs=tok,
        compiler_params=_cparams("parallel"),
        name="ffn_sub%d" % sub,
    )(*streams, mods, nw.reshape(1, D), wi, wo, fnw.reshape(1, D))


def _mix_in_kernel(x_ref, mod_ref, nw_ref, wqkv_ref, wfd_ref, fup_ref, fb_ref, wxl_ref,
                   cw_ref, cb_ref, wrest_ref, *out_refs, seg, need_out):
    if need_out:
        (qdf_ref, kif_ref, kef_ref, qdb_ref, kib_ref, keb_ref, v_ref, decf_ref, decb_ref,
         xc_ref, sg_ref, gy_ref, sa_ref, sb_ref) = out_refs
    else:
        (kef_ref, keb_ref, v_ref, decf_ref, decb_ref, xc_ref) = out_refs
    tm = x_ref.shape[0]
    n_chunks = tm // GLA_CHUNK
    x = x_ref[...]
    shift = mod_ref[3:4, :]
    scale = mod_ref[4:5, :]
    u = (_rms(x) * (nw_ref[...] * (1.0 + scale)) + shift).astype(BF16)
    D = x.shape[1]
    W = wxl_ref.shape[1]
    CT = 2 * LANE
    assert tm % seg == 0 and seg % SUBLANE == 0 and cw_ref.shape[1] == seg

    def kc(h):
        return slice(h * GLA_DK, (h + 1) * GLA_DK)

    fd = _dot(u, wfd_ref[...]).astype(BF16)
    logits = _dot(fd, fup_ref[...]) + fb_ref[...]

    def xl_tile(j):
        return _dot(u, wxl_ref[:, j * CT:(j + 1) * CT])

    def qkv_tile(j):
        return _dot(u, wqkv_ref[:, j * CT:(j + 1) * CT])

    def rest_tile(j):
        return _dot(u, wrest_ref[:, j * CT:(j + 1) * CT])

    sub_pos = lax.broadcasted_iota(jnp.int32, (1, SUBLANE, 1), 1)
    vregs_per_chunk = GLA_CHUNK // SUBLANE

    def chunk_log_decay(r, h, d):
        lg = logits[r, d * GLA_KEY + h * GLA_DK:d * GLA_KEY + (h + 1) * GLA_DK]
        z = lg * LOG2E
        la = (jnp.minimum(z, 0.0) - jnp.log2(1.0 + jnp.exp2(-jnp.abs(z)))) * (1.0 / GLA_TAU)
        la = la.reshape(vregs_per_chunk, SUBLANE, GLA_DK)
        cs = la
        s = 1
        while s < SUBLANE:
            cs = cs + jnp.where(sub_pos >= s, pltpu.roll(cs, s, 1), 0.0)
            s *= 2
        last = cs[:, SUBLANE - 1:SUBLANE, :]
        carry = [jnp.zeros((1, 1, GLA_DK), F32)]
        for g in range(1, vregs_per_chunk):
            carry.append(carry[-1] + last[g - 1:g])
        tot = carry[-1] + last[vregs_per_chunk - 1:]
        return la, cs + jnp.concatenate(carry, axis=0), tot

    def conv(j, xl):
        cs = slice(j * CT, (j + 1) * CT)
        xc = cb_ref[:, cs][None]
        for t in range(CONV_K):
            off = t - CONV_K // 2
            term = xl if off == 0 else pltpu.roll(xl, (-off) % tm, 0)
            xc = xc + term.reshape(tm // seg, seg, CT) * cw_ref[t, :, cs][None]
        xc_ref[:, :, cs] = xc.reshape(tm // SUBLANE, SUBLANE, CT)

    tiles = {}

    def decayed_products(h, c):
        r = slice(c * GLA_CHUNK, (c + 1) * GLA_CHUNK)
        cols = slice((h % hpt) * GLA_DK, (h % hpt + 1) * GLA_DK)
        shape3 = (vregs_per_chunk, SUBLANE, GLA_DK)
        _, b_f, tot_f = chunk_log_decay(r, h, 0)
        la_b, cs_b, tot_b = chunk_log_decay(r, h, 1)
        pre_b = cs_b - la_b
        c_b = tot_b - pre_b
        e_f = tot_f - b_f
        decf_ref[c * SUBLANE:(c + 1) * SUBLANE, kc(h)] = jnp.broadcast_to(jnp.exp2(tot_f[0]), (SUBLANE, GLA_DK))
        decb_ref[c * SUBLANE:(c + 1) * SUBLANE, kc(h)] = jnp.broadcast_to(jnp.exp2(tot_b[0]), (SUBLANE, GLA_DK))

        def put(ref, val):
            ref[r, kc(h)] = val.reshape(GLA_CHUNK, GLA_DK).astype(BF16)

        k = tiles["k", h // hpt][r, cols].reshape(shape3)
        put(kef_ref, k * jnp.exp2(e_f))
        put(keb_ref, k * jnp.exp2(pre_b))
        if need_out:
            put(kif_ref, k * jnp.exp2(-b_f))
            put(kib_ref, k * jnp.exp2(-c_b))
            q = tiles["q", h // hpt][r, cols].reshape(shape3) * (GLA_DK ** -0.5)
            put(qdf_ref, q * jnp.exp2(b_f))
            put(qdb_ref, q * jnp.exp2(c_b))

    n_q = GLA_KEY // CT
    n_v = GLA_VAL // CT
    hpt = CT // GLA_DK
    assert GLA_KEY % CT == 0 and GLA_VAL % CT == 0 and W % CT == 0 and D % CT == 0

    mxu, pending = [], []

    def add_mxu(key, fn):
        mxu.append((key, fn))
        return len(mxu) - 1

    for j in range(W // CT):
        step = add_mxu(("xl", j), functools.partial(xl_tile, j))
        pending.append((step, 7, lambda j=j: conv(j, tiles.pop(("xl", j)))))
    last_qk = 0
    for j in range(n_q):
        last_qk = add_mxu(("k", j), functools.partial(qkv_tile, n_q + j))
    if need_out:
        for j in range(n_q):
            last_qk = add_mxu(("q", j), functools.partial(qkv_tile, j))
    for h in range(GLA_HEADS):
        for c in range(n_chunks):
            pending.append((last_qk, 3 if need_out else 2, functools.partial(decayed_products, h, c)))
    for j in range(n_v):
        step = add_mxu(("v", j), functools.partial(qkv_tile, 2 * n_q + j))

        def vcast(j=j):
            v_ref[:, j * CT:(j + 1) * CT] = tiles.pop(("v", j)).astype(BF16)

        pending.append((step, 1, vcast))
    if need_out:
        acts = ((sg_ref, lambda t: t * _sigmoid(t), 3), (gy_ref, _gelu_tanh, 6),
                (sa_ref, _sigmoid, 2), (sb_ref, _sigmoid, 2))
        per = D // CT
        assert GLA_VAL == D and W == D
        for j in range(len(acts) * per):
            step = add_mxu(("rest", j), functools.partial(rest_tile, j))
            o_ref_, fn, cost = acts[j // per]

            def act(j=j, o_ref_=o_ref_, fn=fn):
                o_ref_[:, (j % per) * CT:(j % per + 1) * CT] = fn(tiles.pop(("rest", j))).astype(BF16)

            pending.append((step, cost, act))

    bulk = [p for p in pending if p[2].__class__ is functools.partial]
    quick = [p for p in pending if p[2].__class__ is not functools.partial]
    quick.sort(key=lambda p: p[0])
    per_step = sum(c for _, c, _ in pending) / len(mxu)
    credit = 0.0
    for i, (key, fn) in enumerate(mxu):
        tiles[key] = fn()
        credit += per_step
        for queue in (quick, bulk):
            while queue and queue[0][0] < i and credit > 0:
                step, cost, task = queue.pop(0)
                task()
                credit -= cost
    for step, cost, task in quick + bulk:
        task()


def _segment_tap_weights(conv_w, seg):
    pos = jnp.arange(seg)[None, :, None]
    off = (jnp.arange(CONV_K) - CONV_K // 2)[:, None, None]
    inside = (pos + off >= 0) & (pos + off < seg)
    return jnp.where(inside, conv_w[:, None, :], 0.0)


def _mix_in(x2, mods, mod_index, nw, wts, *, B, L, tm, seg, need_out):
    rows, D = x2.shape
    W = wts["wxl"].shape[1]
    nt = L // tm
    grid = (B, nt)

    def tok(width):
        return pl.BlockSpec((tm, width), lambda b, j: (b * nt + j, 0))

    dec_spec = pl.BlockSpec((tm // SUBLANE, GLA_KEY), lambda b, j: (b * nt + j, 0))
    lru_spec = pl.BlockSpec((tm // SUBLANE, None, SUBLANE, W), lambda b, j: (j, b, 0, 0))
    tokbf = lambda width: jax.ShapeDtypeStruct((rows, width), BF16)
    dec_shape = jax.ShapeDtypeStruct((rows // SUBLANE, GLA_KEY), F32)
    lru_shape = jax.ShapeDtypeStruct((L // SUBLANE, B, SUBLANE, W), F32)

    n_key = 6 if need_out else 2
    out_shape = [tokbf(GLA_KEY)] * n_key + [tokbf(GLA_VAL), dec_shape, dec_shape, lru_shape]
    out_specs = [tok(GLA_KEY)] * n_key + [tok(GLA_VAL), dec_spec, dec_spec, lru_spec]
    if need_out:
        out_shape += [tokbf(GLA_VAL), tokbf(W), tokbf(D), tokbf(D)]
        out_specs += [tok(GLA_VAL), tok(W), tok(D), tok(D)]

    names = ["wqkv", "wfd", "fup", "fb", "wxl", "cw", "cb", "wrest"]
    w_list = [_segment_tap_weights(wts["cw"], seg) if n == "cw" else wts[n] for n in names]
    in_specs = [pl.BlockSpec((tm, D), lambda b, j: (b * nt + j, 0)),
                pl.BlockSpec((None, N_MOD, D), lambda b, j: (mod_index(b), 0, 0)),
                _const_spec((1, D))] + [_const_spec(w.shape) for w in w_list]
    kern = functools.partial(_mix_in_kernel, seg=seg, need_out=need_out)
    return pl.pallas_call(
        kern,
        out_shape=out_shape,
        grid=grid,
        in_specs=in_specs,
        out_specs=out_specs,
        compiler_params=_cparams("parallel", "parallel"),
        name="mix_in_out" if need_out else "mix_in_ctx",
    )(x2, mods, nw.reshape(1, D), *w_list)


def _lru_kernel(xf_ref, xb_ref, wg_ref, lam_ref, hf0_ref, hb0_ref, *refs, B, tt, need_out):
    if need_out:
        hf_ref, hb_ref, hfl_ref, hbl_ref, af_s, uf_s, ab_s, ub_s, sf_ref, sb_ref = refs
    else:
        hfl_ref, hbl_ref, af_s, uf_s, ab_s, ub_s, sf_ref, sb_ref = refs
    bw = xf_ref.shape[1]
    R = B * SUBLANE

    @pl.when(pl.program_id(1) == 0)
    def _():
        sf_ref[...] = hf0_ref[...]
        sb_ref[...] = hb0_ref[...]

    half_log2 = (-0.5 * LRU_C * LOG2E) * _softplus(-lam_ref[...])
    lane = lax.broadcasted_iota(jnp.int32, (R, bw), 1)
    ones = jnp.where(lane < 2, 1.0, 0.0).astype(BF16)

    def gates(x_ref, d, a_s, u_s, rows):
        xh = x_ref[rows, :]
        lhs = jnp.concatenate([xh.astype(BF16), ones], axis=1)
        z = _dot(lhs, wg_ref[:, 2 * d * bw:(2 * d + 2) * bw])
        t_r = jnp.tanh(z[:, :bw])
        t_i = jnp.tanh(z[:, bw:])
        hl2 = half_log2[d:d + 1, :]
        a = jnp.exp2(hl2 + hl2 * t_r)
        a_s[rows, :] = a
        u_s[rows, :] = _sqrt_nonneg(1.0 - a * a) * (xh + xh * t_i)

    hf = sf_ref[...]
    hb = sb_ref[...]
    n_oct = tt // SUBLANE
    for k in range(n_oct):
        kf, kb = k, n_oct - 1 - k
        gates(xf_ref, 0, af_s, uf_s, slice(kf * R, (kf + 1) * R))
        gates(xb_ref, 1, ab_s, ub_s, slice(kb * R, (kb + 1) * R))
        for t in range(SUBLANE):
            rf = pl.ds(kf * R + t, B, stride=SUBLANE)
            rb = pl.ds(kb * R + SUBLANE - 1 - t, B, stride=SUBLANE)
            hf = af_s[rf, :] * hf + uf_s[rf, :]
            hb = ab_s[rb, :] * hb + ub_s[rb, :]
            if need_out:
                hf_ref[rf, :] = hf
                hb_ref[rb, :] = hb
    sf_ref[...] = hf
    sb_ref[...] = hb
    hfl_ref[...] = hf
    hbl_ref[...] = hb


def _lru(xc, wg, lam, hf0, hb0, *, B, need_out):
    rows, W = xc.shape
    L = rows // B
    tt = SUBLANE * _pick_tile(L // SUBLANE, 32, 1)
    wb = LANE
    assert W // LRU_BLOCKS == wb
    nblk = L // tt
    fwd = pl.BlockSpec((tt * B, wb), lambda w, i: (i, w))
    bwd = pl.BlockSpec((tt * B, wb), lambda w, i: (nblk - 1 - i, w))
    st = pl.BlockSpec((B, wb), lambda w, i: (0, w))
    out_shape = [jax.ShapeDtypeStruct((B, W), F32)] * 2
    out_specs = [st, st]
    if need_out:
        out_shape = [jax.ShapeDtypeStruct((rows, W), F32)] * 2 + out_shape
        out_specs = [fwd, bwd] + out_specs
    kern = functools.partial(_lru_kernel, B=B, tt=tt, need_out=need_out)
    return pl.pallas_call(
        kern,
        out_shape=out_shape,
        grid=(W // wb, nblk),
        in_specs=[fwd, bwd,
                  pl.BlockSpec((None, 2 * wb, 4 * wb), lambda w, i: (w, 0, 0)),
                  pl.BlockSpec((2, wb), lambda w, i: (0, w)),
                  st, st],
        out_specs=out_specs,
        scratch_shapes=[pltpu.VMEM((tt * B, wb), F32)] * 4 + [pltpu.VMEM((B, wb), F32)] * 2,
        compiler_params=_cparams("parallel", "arbitrary"),
        name="lru_out" if need_out else "lru_ctx",
    )(xc, xc, wg, lam, hf0, hb0)


def _gla_kernel(ckef_ref, ckeb_ref, cv_ref, cdf_ref, cdb_ref,
                qdf_ref, kef_ref, vf_ref, df_ref,
                qdb_ref, keb_ref, vb_ref, db_ref,
                of_ref, ob_ref, st_ref):
    C = GLA_CHUNK
    H = GLA_HEADS
    nc = cv_ref.shape[0] // C
    n = vf_ref.shape[0] // C

    def rows(c):
        return pl.ds(pl.multiple_of(c * C, C), C)

    def dec_rows(c):
        return pl.ds(pl.multiple_of(c * SUBLANE, SUBLANE), SUBLANE)

    def kc(h):
        return slice(h * GLA_DK, (h + 1) * GLA_DK)

    def vc(h):
        return slice(h * GLA_DV, (h + 1) * GLA_DV)

    def advance(idx, ke, v, dec8):
        st_ref[idx] = st_ref[idx] * dec8[0:1, :] + _dot_tn(v, ke)

    def attend(idx, qd):
        return _dot_nt(qd, st_ref[idx].astype(BF16)).astype(BF16)

    i = pl.program_id(1)

    @pl.when(i == 0)
    def _():
        st_ref[...] = jnp.zeros(st_ref.shape, F32)

        def ctx_step(s, carry):
            rf, rb = rows(s), rows(nc - 1 - s)
            df8, db8 = cdf_ref[dec_rows(s), :], cdb_ref[dec_rows(nc - 1 - s), :]
            for h in range(H):
                advance(2 * h, ckef_ref[rf, kc(h)], cv_ref[rf, vc(h)], df8[:, kc(h)])
                advance(2 * h + 1, ckeb_ref[rb, kc(h)], cv_ref[rb, vc(h)], db8[:, kc(h)])
            return carry

        lax.fori_loop(0, nc, ctx_step, 0, unroll=4)

    @pl.when(i > 0)
    def _():
        def lat_step(s, carry):
            rf, rb = rows(s), rows(n - 1 - s)
            df8, db8 = df_ref[dec_rows(s), :], db_ref[dec_rows(n - 1 - s), :]
            for h in range(H):
                of_ref[rf, vc(h)] = attend(2 * h, qdf_ref[rf, kc(h)])
                advance(2 * h, kef_ref[rf, kc(h)], vf_ref[rf, vc(h)], df8[:, kc(h)])
                ob_ref[rb, vc(h)] = attend(2 * h + 1, qdb_ref[rb, kc(h)])
                advance(2 * h + 1, keb_ref[rb, kc(h)], vb_ref[rb, vc(h)], db8[:, kc(h)])
            return carry

        lax.fori_loop(0, n, lat_step, 0, unroll=min(n, 16))


def _gla(ctx_t, lat_t, *, B, Tc, T, tl):
    ckef, ckeb, cv, cdf, cdb = ctx_t
    qdf, kef, qdb, keb, v, df, db = lat_t
    nt = T // tl

    def cblk(rows_, width):
        return pl.BlockSpec((rows_, width), lambda b, i: (b, 0))

    def fblk(rows_, width):
        return pl.BlockSpec((rows_, width), lambda b, i: (b * nt + jnp.maximum(i - 1, 0), 0))

    def bblk(rows_, width):
        return pl.BlockSpec((rows_, width), lambda b, i: (b * nt + nt - 1 - jnp.maximum(i - 1, 0), 0))

    o_shape = jax.ShapeDtypeStruct((B * T, GLA_VAL), BF16)
    return pl.pallas_call(
        _gla_kernel,
        out_shape=[o_shape, o_shape],
        grid=(B, 1 + nt),
        in_specs=[cblk(Tc, GLA_KEY), cblk(Tc, GLA_KEY), cblk(Tc, GLA_VAL),
                  cblk(Tc // SUBLANE, GLA_KEY), cblk(Tc // SUBLANE, GLA_KEY),
                  fblk(tl, GLA_KEY), fblk(tl, GLA_KEY), fblk(tl, GLA_VAL), fblk(tl // SUBLANE, GLA_KEY),
                  bblk(tl, GLA_KEY), bblk(tl, GLA_KEY), bblk(tl, GLA_VAL), bblk(tl // SUBLANE, GLA_KEY)],
        out_specs=[fblk(tl, GLA_VAL), bblk(tl, GLA_VAL)],
        scratch_shapes=[pltpu.VMEM((2 * GLA_HEADS, GLA_DV, GLA_DK), F32)],
        compiler_params=_cparams("parallel", "arbitrary"),
        name="gla",
    )(ckef, ckeb, cv, cdf, cdb, qdf, kef, v, df, qdb, keb, v, db)


def _mix_out_kernel(of_ref, ob_ref, qdf_ref, kif_ref, qdb_ref, kib_ref, v_ref,
                    sg_ref, gnw_ref, hf_ref, hb_ref, gy_ref, sa_ref, sb_ref,
                    wgla_ref, wlru_ref, wo_ref, o_ref, oi_ref):
    tm, W = gy_ref.shape
    hsum = (hf_ref[...] + hb_ref[...]).reshape(tm, W)
    hl = (hsum * gy_ref[...].astype(F32)).astype(BF16)
    y_lru = _dot(hl, wlru_ref[...])
    R = _pick_tile(tm, 2 * LANE, GLA_CHUNK)
    ri = lax.broadcasted_iota(jnp.int32, (R, R), 0)
    ci = lax.broadcasted_iota(jnp.int32, (R, R), 1)
    same = (ri // GLA_CHUNK) == (ci // GLA_CHUNK)
    fwd_ok = same & (ci <= ri)
    bwd_ok = same & (ci > ri)
    og = []
    for h in range(GLA_HEADS):
        ks = slice(h * GLA_DK, (h + 1) * GLA_DK)
        cs = slice(h * GLA_DV, (h + 1) * GLA_DV)
        for c in range(tm // R):
            r = slice(c * R, (c + 1) * R)
            sf = _dot_nt(qdf_ref[r, ks], kif_ref[r, ks])
            sb = _dot_nt(qdb_ref[r, ks], kib_ref[r, ks])
            sc = jnp.where(fwd_ok, sf, jnp.where(bwd_ok, sb, 0.0))
            oi_ref[r, cs] = _dot(sc.astype(BF16), v_ref[r, cs])
        o = oi_ref[:, cs] + of_ref[:, cs].astype(F32) + ob_ref[:, cs].astype(F32)
        og.append((_rms(o) * gnw_ref[...] * sg_ref[:, cs].astype(F32)).astype(BF16))
    y_gla = _dot(jnp.concatenate(og, axis=1), wgla_ref[...])
    merged = sa_ref[...].astype(F32) * y_gla + sb_ref[...].astype(F32) * y_lru
    o_ref[...] = _dot(merged.astype(BF16), wo_ref[...]).astype(o_ref.dtype)


def _mix_out(of, ob, qdf, kif, qdb, kib, v, sg, gnw, hf, hb, gy, sa, sb, wgla, wlru, wo, *, B, T, tm):
    rows = of.shape[0]
    W, D = wlru.shape
    nt = T // tm
    tok = lambda width: pl.BlockSpec((tm, width), lambda b, j: (b * nt + j, 0))
    lru = pl.BlockSpec((tm // SUBLANE, None, SUBLANE, W), lambda b, j: (j, b, 0, 0))
    return pl.pallas_call(
        _mix_out_kernel,
        out_shape=jax.ShapeDtypeStruct((rows, D), BF16),
        grid=(B, nt),
        in_specs=[tok(GLA_VAL), tok(GLA_VAL),
                  tok(GLA_KEY), tok(GLA_KEY), tok(GLA_KEY), tok(GLA_KEY), tok(GLA_VAL),
                  tok(GLA_VAL), _const_spec((1, GLA_DV)),
                  lru, lru, tok(W), tok(D), tok(D),
                  _const_spec(wgla.shape), _const_spec(wlru.shape), _const_spec(wo.shape)],
        out_specs=tok(D),
        scratch_shapes=[pltpu.VMEM((tm, GLA_VAL), F32)],
        compiler_params=_cparams("parallel", "parallel"),
        name="mix_out",
    )(of, ob, qdf, kif, qdb, kib, v, sg, gnw.reshape(1, GLA_DV), hf, hb, gy, sa, sb, wgla, wlru, wo)


def _prep_mixer_weights(w_in, fup, fb, conv_w, conv_b, wr, br, wi, bi, lam):
    D = w_in.shape[0]
    o = 0
    wqkv = w_in[:, o:o + 2 * GLA_KEY + GLA_VAL]
    o += 2 * GLA_KEY + GLA_VAL
    w_g = w_in[:, o:o + GLA_VAL]
    o += GLA_VAL
    wfd = w_in[:, o:o + 2 * GLA_RANK]
    o += 2 * GLA_RANK
    W = lam.shape[1]
    wxl = w_in[:, o:o + W]
    o += W
    w_tail = w_in[:, o:]
    wfd_p = jnp.zeros((D, LANE), F32).at[:, :2 * GLA_RANK].set(wfd)
    fup_p = jnp.zeros((LANE, 2 * GLA_KEY), F32)
    fup_p = fup_p.at[:GLA_RANK, :GLA_KEY].set(fup[0]).at[GLA_RANK:2 * GLA_RANK, GLA_KEY:].set(fup[1])
    wg = jnp.concatenate([wr[0], wi[0], wr[1], wi[1]], axis=-1)
    bw = W // LRU_BLOCKS
    bg = 0.5 * jnp.stack([br[0], bi[0], br[1], bi[1]])
    bg = bg.reshape(4, LRU_BLOCKS, bw).transpose(1, 0, 2).reshape(LRU_BLOCKS, 1, 4 * bw)
    bg_hi = bg.astype(BF16)
    bg_lo = (bg - bg_hi.astype(F32)).astype(BF16)
    wg_aug = jnp.concatenate([wg.astype(BF16), bg_hi, bg_lo,
                              jnp.zeros((LRU_BLOCKS, bw - 2, 4 * bw), BF16)], axis=1)
    return {
        "wqkv": wqkv.astype(BF16),
        "wfd": wfd_p.astype(BF16),
        "fup": fup_p.astype(BF16),
        "fb": jnp.concatenate([fb[0], fb[1]]).reshape(1, 2 * GLA_KEY),
        "wxl": wxl.astype(BF16),
        "cw": 0.5 * conv_w,
        "cb": 0.5 * conv_b.reshape(1, W),
        "wg": wg_aug,
        "lam": lam,
        "wrest": jnp.concatenate([w_g, w_tail], axis=1).astype(BF16),
    }


def kernel(x, c, ctx, c_ctx, w_ada, b_ada, norm_w, ffn1_wi, ffn1_wo, ffn2_wi, ffn2_wo, w_in, gla_fup, gla_fb, gla_norm_w, conv_w, conv_b, lru_wr, lru_br, lru_wi, lru_bi, lru_lam, w_out_gla, w_out_lru, w_o, final_norm_w):
    B, T, D = x.shape
    Tc = ctx.shape[1]
    depth = w_ada.shape[0]
    assert depth == 1, "single trunk layer"
    assert B % SUBLANE == 0 and T % GRID_W == 0 and Tc % GLA_CHUNK == 0
    l = 0

    R = -(-(B + 1) // SUBLANE) * SUBLANE
    cs = jnp.zeros((R, D), F32).at[:B].set(c).at[B].set(c_ctx)
    mods = _ada(cs, w_ada, b_ada, l).reshape(R, N_MOD, D)

    x2 = x.reshape(B * T, D)
    c2 = ctx.reshape(B * Tc, D)
    tm_lat = _pick_tile(T, 512, GRID_W)
    wi1, wo1 = ffn1_wi[l].astype(BF16), ffn1_wo[l].astype(BF16)
    wi2, wo2 = ffn2_wi[l].astype(BF16), ffn2_wo[l].astype(BF16)
    tm_ffn = _pick_tile(T, 1024, GRID_W)
    nt_ffn = T // tm_ffn
    tm_cf = _pick_tile(B * Tc, 1024, SUBLANE)

    h = _ffn(x2, mods, lambda i: i // nt_ffn, norm_w[l, 0], wi1, wo1, final_norm_w,
             sub=0, tm=tm_ffn, final_norm=False)
    hc = _ffn(c2, mods, lambda i: B, norm_w[l, 0], wi1, wo1, final_norm_w,
              sub=0, tm=tm_cf, final_norm=False)

    wts = _prep_mixer_weights(w_in[l], gla_fup[l], gla_fb[l], conv_w[l], conv_b[l],
                              lru_wr[l], lru_br[l], lru_wi[l], lru_bi[l], lru_lam[l])
    W = lru_lam.shape[-1]
    ctx_o = _mix_in(hc, mods, lambda b: B, norm_w[l, 1], wts,
                    B=B, L=Tc, tm=Tc, seg=Tc, need_out=False)
    ckef, ckeb, cv, cdf, cdb, cxc = ctx_o
    lat_o = _mix_in(h, mods, lambda b: b, norm_w[l, 1], wts,
                    B=B, L=T, tm=tm_lat, seg=GRID_W, need_out=True)
    (qdf, kif, kef, qdb, kib, keb, v, df, db, xc, sg, gy, sa, sb) = lat_o

    zero = jnp.zeros((B, W), F32)
    hf_c, hb_c = _lru(cxc.reshape(Tc * B, W), wts["wg"], wts["lam"], zero, zero,
                      B=B, need_out=False)
    hf, hb, _, _ = _lru(xc.reshape(T * B, W), wts["wg"], wts["lam"], hf_c, hb_c,
                        B=B, need_out=True)

    of, ob = _gla((ckef, ckeb, cv, cdf, cdb), (qdf, kef, qdb, keb, v, df, db),
                  B=B, Tc=Tc, T=T, tl=_pick_tile(T, 1024, GLA_CHUNK))

    oct_ = lambda t: t.reshape(T // SUBLANE, B, SUBLANE, W)
    y = _mix_out(of, ob, qdf, kif, qdb, kib, v, sg, gla_norm_w[l], oct_(hf), oct_(hb), gy, sa, sb,
                 w_out_gla[l].astype(BF16), w_out_lru[l].astype(BF16), w_o[l].astype(BF16),
                 B=B, T=T, tm=tm_lat)

    out = _ffn(h, mods, lambda i: i // nt_ffn, norm_w[l, 2], wi2, wo2, final_norm_w,
               sub=2, tm=tm_ffn, final_norm=True, branch=y, branch_sub=1)
    return out.reshape(B, T, D)
```
